```python
import jax, jax.numpy as jnp
from jax import lax
import numpy as np

D_MODEL = 2048
BATCH = 4
SEQ = 8192
DEPTH = 1

D_PLE = 256
RMS_EPS = 1e-6
POOL_WINDOWS = (2, 4, 8, 16)
N_POOL_GROUPS = 4
POOL_GROUP_DIM = 128
D_POOL = N_POOL_GROUPS * POOL_GROUP_DIM
N_HEADS = 12
QK_NOPE_DIM = 128
QK_ROPE_DIM = 64
QK_HEAD_DIM = QK_NOPE_DIM + QK_ROPE_DIM
V_HEAD_DIM = 128
Q_LORA_RANK = 512
KV_LORA_RANK = 512
D_ATTN = N_HEADS * V_HEAD_DIM
ROPE_THETA = 10000.0
Q_BLOCK = 128
D_MIX = D_POOL + D_ATTN
D_IN = D_POOL + Q_LORA_RANK + KV_LORA_RANK + QK_ROPE_DIM
PEER_HEADS = 8
PEER_N_KEYS = 128
PEER_N_EXPERTS = PEER_N_KEYS * PEER_N_KEYS
PEER_TOPK = 16
PEER_HALF_DIM = 128
PEER_QUERY_DIM = 2 * PEER_HALF_DIM
PEER_TOKEN_CHUNK = 128
PEER_V_SCALE = 0.5

kernel_name = "hymba_pool_mla_peer_block"


def rmsnorm(x, g):
    xf = x.astype(jnp.float32)
    y = xf * lax.rsqrt(jnp.mean(xf * xf, axis=-1, keepdims=True) + RMS_EPS)
    return (y * g.astype(jnp.float32)).astype(x.dtype)


def rope_tables(positions):
    inv_freq = ROPE_THETA ** (-jnp.arange(0, QK_ROPE_DIM, 2, dtype=jnp.float32) / QK_ROPE_DIM)
    ang = positions.astype(jnp.float32)[..., None] * inv_freq
    return jnp.cos(ang), jnp.sin(ang)


def apply_rope(x, cos, sin):
    half = x.shape[-1] // 2
    x1, x2 = x[..., :half], x[..., half:]
    c = cos.astype(x.dtype)
    s = sin.astype(x.dtype)
    return jnp.concatenate([x1 * c - x2 * s, x2 * c + x1 * s], axis=-1)


def pool_mixer(xp, pool_w, pool_scale):
    b, s, _ = xp.shape
    xg = xp.astype(jnp.float32).reshape(b, s, N_POOL_GROUPS, POOL_GROUP_DIM)
    csum = jnp.cumsum(xg, axis=1)
    t = jnp.arange(s)
    outs = []
    for gi, w in enumerate(POOL_WINDOWS):
        c = csum[:, :, gi]
        lag = jnp.pad(c[:, :s - w], ((0, 0), (w, 0), (0, 0)))
        cnt = jnp.minimum(t + 1, w).astype(jnp.float32)[None, :, None]
        outs.append((c - lag) / cnt - xg[:, :, gi])
    pooled = jnp.stack(outs, axis=2).astype(xp.dtype)
    mixed = jnp.einsum('bsgc,gcd->bsgd', pooled, pool_w).reshape(b, s, D_POOL)
    return mixed * pool_scale


def mla_attention(q_c, kv_c, k_r, q_norm_g, w_uq, kv_norm_g, w_ukv, cos, sin):
    b, s, _ = q_c.shape
    q = (rmsnorm(q_c, q_norm_g) @ w_uq).reshape(b, s, N_HEADS, QK_HEAD_DIM)
    q_nope = q[..., :QK_NOPE_DIM]
    q_rope = apply_rope(q[..., QK_NOPE_DIM:], cos[:, :, None], sin[:, :, None])
    kv = (rmsnorm(kv_c, kv_norm_g) @ w_ukv).reshape(b, s, N_HEADS, QK_NOPE_DIM + V_HEAD_DIM)
    k_nope = kv[..., :QK_NOPE_DIM]
    v = kv[..., QK_NOPE_DIM:]
    k_rope = apply_rope(k_r, cos, sin)
    scale = QK_HEAD_DIM ** -0.5
    outs = []
    for qb in range(s // Q_BLOCK):
        q0 = qb * Q_BLOCK
        kend = q0 + Q_BLOCK
        sc = (jnp.einsum('bqhd,bkhd->bhqk', q_nope[:, q0:kend], k_nope[:, :kend])
              + jnp.einsum('bqhr,bkr->bhqk', q_rope[:, q0:kend], k_rope[:, :kend]))
        sc = sc.astype(jnp.float32) * scale
        causal = jnp.arange(kend)[None, :] <= (q0 + jnp.arange(Q_BLOCK))[:, None]
        sc = jnp.where(causal, sc, jnp.float32(-1e30))
        pr = jax.nn.softmax(sc, axis=-1).astype(v.dtype)
        outs.append(jnp.einsum('bhqk,bkhd->bqhd', pr, v[:, :kend]))
    o = jnp.concatenate(outs, axis=1)
    return o.reshape(b, s, D_ATTN)


def peer_ffn(xn, wq, sub_keys, u, v):
    b, s, d = xn.shape
    t = b * s
    xt = xn.reshape(t, d)
    q = (xt @ wq).reshape(t, PEER_HEADS, 2, PEER_HALF_DIM)
    sc = jnp.einsum('thpc,hpnc->thpn', q, sub_keys).astype(jnp.float32)
    top_v, top_i = lax.top_k(sc, PEER_TOPK)
    kk = PEER_TOPK * PEER_TOPK
    cand_v = (top_v[:, :, 0, :, None] + top_v[:, :, 1, None, :]).reshape(t, PEER_HEADS, kk)
    cand_id = (top_i[:, :, 0, :, None] * PEER_N_KEYS + top_i[:, :, 1, None, :]).reshape(t, PEER_HEADS, kk)
    best_v, best_pos = lax.top_k(cand_v, PEER_TOPK)
    expert = jnp.take_along_axis(cand_id, best_pos, axis=-1)
    gate = jax.nn.softmax(best_v, axis=-1).astype(xn.dtype)
    n_chunks = t // PEER_TOKEN_CHUNK

    def chunk_fn(args):
        xc, ec, gc = args
        hc = jnp.einsum('cd,chkd->chk', xc, u[ec])
        ac = gc * jax.nn.gelu(hc)
        return jnp.einsum('chk,chkd->cd', ac, v[ec])

    out = lax.map(chunk_fn, (
        xt.reshape(n_chunks, PEER_TOKEN_CHUNK, d),
        expert.reshape(n_chunks, PEER_TOKEN_CHUNK, PEER_HEADS, PEER_TOPK),
        gate.reshape(n_chunks, PEER_TOKEN_CHUNK, PEER_HEADS, PEER_TOPK)))
    return out.reshape(b, s, d)


def setup_inputs(seed: int = 0) -> dict:
    key = jax.random.key(seed)
    ks = jax.random.split(key, 24)
    f32 = jnp.float32
    nrm = lambda k, shape, sc: jax.random.normal(k, shape, f32) * sc
    gain = lambda k, shape: 1.0 + 0.02 * jax.random.normal(k, shape, f32)
    L = DEPTH
    return {
        "x": jax.random.normal(ks[0], (BATCH, SEQ, D_MODEL), f32),
        "p": jax.random.normal(ks[1], (DEPTH, BATCH, SEQ, D_PLE), f32),
        "positions": jnp.broadcast_to(jnp.arange(SEQ, dtype=jnp.int32), (BATCH, SEQ)),
        "mix_norm_g": gain(ks[2], (L, D_MODEL)),
        "w_in": nrm(ks[3], (L, D_MODEL, D_IN), D_MODEL ** -0.5),
        "pool_w": nrm(ks[4], (L, N_POOL_GROUPS, POOL_GROUP_DIM, POOL_GROUP_DIM), POOL_GROUP_DIM ** -0.5),
        "pool_scale": gain(ks[5], (L, D_POOL)),
        "q_norm_g": gain(ks[6], (L, Q_LORA_RANK)),
        "w_uq": nrm(ks[7], (L, Q_LORA_RANK, N_HEADS * QK_HEAD_DIM), Q_LORA_RANK ** -0.5),
        "kv_norm_g": gain(ks[8], (L, KV_LORA_RANK)),
        "w_ukv": nrm(ks[9], (L, KV_LORA_RANK, N_HEADS * (QK_NOPE_DIM + V_HEAD_DIM)), KV_LORA_RANK ** -0.5),
        "w_o": nrm(ks[10], (L, D_MIX, D_MODEL), D_MIX ** -0.5),
        "ffn_norm_g": gain(ks[11], (L, D_MODEL)),
        "peer_wq": nrm(ks[12], (L, D_MODEL, PEER_HEADS * PEER_QUERY_DIM), D_MODEL ** -0.5),
        "peer_sub_keys": nrm(ks[13], (L, PEER_HEADS, 2, PEER_N_KEYS, PEER_HALF_DIM), PEER_HALF_DIM ** -0.5),
        "peer_u": nrm(ks[14], (L, PEER_N_EXPERTS, D_MODEL), D_MODEL ** -0.5),
        "peer_v": nrm(ks[15], (L, PEER_N_EXPERTS, D_MODEL), PEER_V_SCALE),
        "ple_norm_g": gain(ks[16], (L, D_MODEL)),
        "ple_w_gate": nrm(ks[17], (L, D_MODEL, D_MODEL), D_MODEL ** -0.5),
        "ple_w_proj": nrm(ks[18], (L, D_PLE, D_MODEL), D_PLE ** -0.5),
        "final_norm_g": gain(ks[19], (D_MODEL,)),
    }


def reference(x, p, positions, mix_norm_g, w_in, pool_w, pool_scale, q_norm_g, w_uq,
              kv_norm_g, w_ukv, w_o, ffn_norm_g, peer_wq, peer_sub_keys, peer_u, peer_v,
              ple_norm_g, ple_w_gate, ple_w_proj, final_norm_g):
    cos, sin = rope_tables(positions)
    h = x
    o1 = D_POOL
    o2 = o1 + Q_LORA_RANK
    o3 = o2 + KV_LORA_RANK
    for i in range(DEPTH):
        z = rmsnorm(h, mix_norm_g[i]) @ w_in[i]
        y_pool = pool_mixer(z[..., :o1], pool_w[i], pool_scale[i])
        y_attn = mla_attention(z[..., o1:o2], z[..., o2:o3], z[..., o3:],
                               q_norm_g[i], w_uq[i], kv_norm_g[i], w_ukv[i], cos, sin)
        h = h + jnp.concatenate([y_pool, y_attn], axis=-1) @ w_o[i]
        h = h + peer_ffn(rmsnorm(h, ffn_norm_g[i]), peer_wq[i], peer_sub_keys[i], peer_u[i], peer_v[i])
        gate = jax.nn.sigmoid(rmsnorm(h, ple_norm_g[i]) @ ple_w_gate[i])
        h = h + gate * (p[i] @ ple_w_proj[i])
    return rmsnorm(h, final_norm_g)
```

```python
import functools

import jax
import jax.numpy as jnp
from jax import lax
from jax.experimental import pallas as pl
from jax.experimental.pallas import tpu as pltpu

D_MODEL = 2048
D_PLE = 256
RMS_EPS = 1e-6
POOL_WINDOWS = (2, 4, 8, 16)
POOL_GROUP_DIM = 128
D_POOL = 512
POOL_HALO = 16
N_HEADS = 12
QK_NOPE_DIM = 128
QK_ROPE_DIM = 64
QK_HEAD_DIM = QK_NOPE_DIM + QK_ROPE_DIM
V_HEAD_DIM = 128
LORA = 512
D_ATTN = N_HEADS * V_HEAD_DIM
ROPE_THETA = 10000.0
QK_PAD = 256
PEER_HEADS = 8
PEER_N_KEYS = 128
PEER_TOPK = 16
PEER_HALF = 128
PEER_SEL = PEER_HEADS * PEER_TOPK

LANES = 128
VMEM_LIMIT = 56 * 1024 * 1024

F32 = jnp.float32
BF16 = jnp.bfloat16
NEG = -1e30


def _rms(x, g):
    return x * lax.rsqrt(jnp.mean(x * x, axis=-1, keepdims=True) + RMS_EPS) * g


def _const_spec(shape):
    nd = len(shape)
    return pl.BlockSpec(shape, lambda *_: (0,) * nd, pipeline_mode=pl.Buffered(1))


def _mix_in_kernel(x_ref, cos_ref, sin_ref, g_ref, w_in_ref, pool_w_ref, pool_s_ref,
                   qg_ref, wqn_ref, wqa_ref, wqb_ref, kvg_ref, wk_ref, wv_ref,
                   ypool_ref, q_ref, k_ref, v_ref, halo_ref):
    tm = x_ref.shape[0]
    si = pl.program_id(1)
    xn = _rms(x_ref[...], g_ref[...]).astype(BF16)
    z = jnp.dot(xn, w_in_ref[...], preferred_element_type=F32)
    zp = z[:, :D_POOL]
    cosb = cos_ref[...]
    sinb = sin_ref[...]

    @pl.when(si == 0)
    def _():
        halo_ref[0:POOL_HALO, :] = jnp.zeros((POOL_HALO, D_POOL), F32)

    halo_ref[POOL_HALO:POOL_HALO + tm, :] = zp
    pos = si * tm + lax.broadcasted_iota(jnp.int32, (tm, 1), 0)
    for gi, w in enumerate(POOL_WINDOWS):
        c0 = gi * POOL_GROUP_DIM
        cur = zp[:, c0:c0 + POOL_GROUP_DIM]
        acc = cur
        for j in range(1, w):
            acc = acc + halo_ref[POOL_HALO - j:POOL_HALO - j + tm, c0:c0 + POOL_GROUP_DIM]
        cnt = jnp.minimum(pos + 1, w).astype(F32)
        pooled = (acc / cnt - cur).astype(BF16)
        mixed = jnp.dot(pooled, pool_w_ref[gi], preferred_element_type=F32)
        ypool_ref[:, c0:c0 + POOL_GROUP_DIM] = (
            mixed * pool_s_ref[:, c0:c0 + POOL_GROUP_DIM]).astype(BF16)
    halo_ref[0:POOL_HALO, :] = halo_ref[tm:tm + POOL_HALO, :]

    qn = _rms(z[:, D_POOL:D_POOL + LORA], qg_ref[...]).astype(BF16)
    q_nope = jnp.dot(qn, wqn_ref[...], preferred_element_type=F32)
    q_a = jnp.dot(qn, wqa_ref[...], preferred_element_type=F32)
    q_b = jnp.dot(qn, wqb_ref[...], preferred_element_type=F32)
    for h in range(N_HEADS):
        q_ref[:, h * QK_PAD:h * QK_PAD + QK_NOPE_DIM] = (
            q_nope[:, h * QK_NOPE_DIM:(h + 1) * QK_NOPE_DIM].astype(BF16))
    for j in range(N_HEADS // 2):
        rp = (q_a[:, j * LANES:(j + 1) * LANES] * cosb
              + q_b[:, j * LANES:(j + 1) * LANES] * sinb).astype(BF16)
        for h in (2 * j, 2 * j + 1):
            q_ref[:, h * QK_PAD + QK_NOPE_DIM:(h + 1) * QK_PAD] = rp

    kvn = _rms(z[:, D_POOL + LORA:D_POOL + 2 * LORA], kvg_ref[...]).astype(BF16)
    k_nope = jnp.dot(kvn, wk_ref[...], preferred_element_type=F32)
    v = jnp.dot(kvn, wv_ref[...], preferred_element_type=F32)
    v_ref[...] = v.astype(BF16)
    o = D_POOL + 2 * LORA
    kr = z[:, o:o + LANES] * cosb + z[:, o + LANES:o + 2 * LANES] * sinb
    lane = lax.broadcasted_iota(jnp.int32, (tm, LANES), 1)
    even = (lane // (QK_ROPE_DIM // 2)) % 2 == 0
    kr_even = jnp.where(even, kr, 0.0).astype(BF16)
    kr_odd = jnp.where(even, 0.0, kr).astype(BF16)
    for h in range(N_HEADS):
        k_ref[:, h * QK_PAD:h * QK_PAD + QK_NOPE_DIM] = (
            k_nope[:, h * QK_NOPE_DIM:(h + 1) * QK_NOPE_DIM].astype(BF16))
        k_ref[:, h * QK_PAD + QK_NOPE_DIM:(h + 1) * QK_PAD] = kr_even if h % 2 == 0 else kr_odd


def _mix_in(x, cos4, sin4, g, w_in_ext, pool_w, pool_s, qg, wqn, wqa, wqb, kvg, wk, wv, tm):
    b, s, d = x.shape
    tok = lambda width: pl.BlockSpec((None, tm, width), lambda bi, si: (bi, si, 0))
    return pl.pallas_call(
        _mix_in_kernel,
        grid=(b, s // tm),
        in_specs=[tok(d), tok(LANES), tok(LANES), _const_spec(g.shape), _const_spec(w_in_ext.shape),
                  _const_spec(pool_w.shape), _const_spec(pool_s.shape), _const_spec(qg.shape),
                  _const_spec(wqn.shape), _const_spec(wqa.shape), _const_spec(wqb.shape),
                  _const_spec(kvg.shape), _const_spec(wk.shape), _const_spec(wv.shape)],
        out_specs=[tok(D_POOL), tok(N_HEADS * QK_PAD), tok(N_HEADS * QK_PAD), tok(D_ATTN)],
        out_shape=[jax.ShapeDtypeStruct((b, s, D_POOL), BF16),
                   jax.ShapeDtypeStruct((b, s, N_HEADS * QK_PAD), BF16),
                   jax.ShapeDtypeStruct((b, s, N_HEADS * QK_PAD), BF16),
                   jax.ShapeDtypeStruct((b, s, D_ATTN), BF16)],
        scratch_shapes=[pltpu.VMEM((POOL_HALO + tm, D_POOL), F32)],
        compiler_params=pltpu.CompilerParams(
            dimension_semantics=("parallel", "arbitrary"), vmem_limit_bytes=VMEM_LIMIT),
        name="mix_in",
    )(x, cos4, sin4, g, w_in_ext, pool_w, pool_s, qg, wqn, wqa, wqb, kvg, wk, wv)


def _attn_kernel(q_ref, k_ref, v_ref, o_ref, m_ref, l_ref, acc_ref):
    tq = q_ref.shape[0]
    tk = k_ref.shape[0]
    qi = pl.program_id(1)
    kj = pl.program_id(2)
    scale = QK_HEAD_DIM ** -0.5

    @pl.when(kj == 0)
    def _():
        m_ref[...] = jnp.full(m_ref.shape, NEG, F32)
        l_ref[...] = jnp.zeros(l_ref.shape, F32)
        acc_ref[...] = jnp.zeros(acc_ref.shape, F32)

    def step(masked):
        if masked:
            row = lax.broadcasted_iota(jnp.int32, (tq, tk), 0)
            col = lax.broadcasted_iota(jnp.int32, (tq, tk), 1)
            keep = col <= row
        for h in range(N_HEADS):
            q = q_ref[:, h * QK_PAD:(h + 1) * QK_PAD]
            k = k_ref[:, h * QK_PAD:(h + 1) * QK_PAD]
            s = lax.dot_general(q, k, (((1,), (1,)), ((), ())),
                                preferred_element_type=F32) * scale
            if masked:
                s = jnp.where(keep, s, NEG)
            m_prev = m_ref[h]
            m_new = jnp.maximum(m_prev, jnp.max(s, axis=-1, keepdims=True))
            p = jnp.exp(s - m_new)
            alpha = jnp.exp(m_prev - m_new)
            l_ref[h] = alpha * l_ref[h] + jnp.sum(p, axis=-1, keepdims=True)
            hs = slice(h * V_HEAD_DIM, (h + 1) * V_HEAD_DIM)
            acc_ref[:, hs] = alpha * acc_ref[:, hs] + jnp.dot(
                p.astype(BF16), v_ref[:, hs], preferred_element_type=F32)
            m_ref[h] = m_new

    @pl.when(kj < qi)
    def _():
        step(False)

    @pl.when(kj == qi)
    def _():
        step(True)
        for h in range(N_HEADS):
            hs = slice(h * V_HEAD_DIM, (h + 1) * V_HEAD_DIM)
            o_ref[:, hs] = (acc_ref[:, hs] / l_ref[h]).astype(BF16)


def _attention(q, k, v, tq):
    b, s, _ = q.shape
    n = s // tq
    kv_map = lambda bi, qi, kj: (bi, jnp.minimum(kj, qi), 0)
    return pl.pallas_call(
        _attn_kernel,
        grid=(b, n, n),
        in_specs=[pl.BlockSpec((None, tq, N_HEADS * QK_PAD), lambda bi, qi, kj: (bi, qi, 0)),
                  pl.BlockSpec((None, tq, N_HEADS * QK_PAD), kv_map),
                  pl.BlockSpec((None, tq, D_ATTN), kv_map)],
        out_specs=pl.BlockSpec((None, tq, D_ATTN), lambda bi, qi, kj: (bi, qi, 0)),
        out_shape=jax.ShapeDtypeStruct((b, s, D_ATTN), BF16),
        scratch_shapes=[pltpu.VMEM((N_HEADS, tq, 1), F32), pltpu.VMEM((N_HEADS, tq, 1), F32),
                        pltpu.VMEM((tq, D_ATTN), F32)],
        compiler_params=pltpu.CompilerParams(
            dimension_semantics=("parallel", "parallel", "arbitrary"),
            vmem_limit_bytes=VMEM_LIMIT),
        name="attention",
    )(q, k, v)


def _peer_q_kernel(x_ref, yp_ref, ya_ref, wop_ref, woa_ref, g_ref, wq_ref, sk_ref,
                   h1_ref, xn_ref, sc_ref):
    h1 = (x_ref[...]
          + jnp.dot(yp_ref[...], wop_ref[...], preferred_element_type=F32)
          + jnp.dot(ya_ref[...], woa_ref[...], preferred_element_type=F32))
    h1_ref[...] = h1
    xn = _rms(h1, g_ref[...])
    xn_ref[...] = xn
    q = jnp.dot(xn.astype(BF16), wq_ref[...], preferred_element_type=F32).astype(BF16)
    for hp in range(2 * PEER_HEADS):
        sc_ref[hp] = lax.dot_general(sk_ref[hp], q[:, hp * PEER_HALF:(hp + 1) * PEER_HALF],
                                     (((1,), (1,)), ((), ())), preferred_element_type=F32)


def _peer_q(x2, ypool2, yattn2, wop, woa, g, wq, sk, tm):
    t, d = x2.shape
    row = lambda width: pl.BlockSpec((tm, width), lambda i: (i, 0))
    return pl.pallas_call(
        _peer_q_kernel,
        grid=(t // tm,),
        in_specs=[row(d), row(D_POOL), row(D_ATTN), _const_spec(wop.shape), _const_spec(woa.shape),
                  _const_spec(g.shape), _const_spec(wq.shape), _const_spec(sk.shape)],
        out_specs=[row(d), row(d),
                   pl.BlockSpec((2 * PEER_HEADS, PEER_N_KEYS, tm), lambda i: (0, 0, i))],
        out_shape=[jax.ShapeDtypeStruct((t, d), F32), jax.ShapeDtypeStruct((t, d), F32),
                   jax.ShapeDtypeStruct((2 * PEER_HEADS, PEER_N_KEYS, t), F32)],
        compiler_params=pltpu.CompilerParams(
            dimension_semantics=("parallel",), vmem_limit_bytes=VMEM_LIMIT),
        name="peer_q",
    )(x2, ypool2, yattn2, wop, woa, g, wq, sk)


def _extract_topk(vals, payload, k):
    n = vals.shape[0]
    rows = lax.broadcasted_iota(jnp.int32, vals.shape, 0)
    out_v, out_p = [], []
    for _ in range(k):
        m = jnp.max(vals, axis=0, keepdims=True)
        pos = jnp.min(jnp.where(vals == m, rows, n), axis=0, keepdims=True)
        hit = rows == pos
        out_v.append(m)
        out_p.append(jnp.max(jnp.where(hit, payload, -1), axis=0, keepdims=True))
        vals = jnp.where(hit, -jnp.inf, vals)
    return jnp.concatenate(out_v, axis=0), jnp.concatenate(out_p, axis=0)


def _peer_topk_kernel(sc_ref, idx_ref, gate_ref):
    tl = sc_ref.shape[-1]
    key_ids = lax.broadcasted_iota(jnp.int32, (PEER_N_KEYS, tl), 0)
    for h in range(PEER_HEADS):
        v0, i0 = _extract_topk(sc_ref[2 * h], key_ids, PEER_TOPK)
        v1, i1 = _extract_topk(sc_ref[2 * h + 1], key_ids, PEER_TOPK)
        cand_v = jnp.concatenate([v0[a:a + 1] + v1 for a in range(PEER_TOPK)], axis=0)
        cand_i = jnp.concatenate([i0[a:a + 1] * PEER_N_KEYS + i1 for a in range(PEER_TOPK)], axis=0)
        best_v, best_i = _extract_topk(cand_v, cand_i, PEER_TOPK)
        e = jnp.exp(best_v - best_v[0:1])
        gate = e / jnp.sum(e, axis=0, keepdims=True)
        idx_ref[h * PEER_TOPK:(h + 1) * PEER_TOPK, :] = best_i
        gate_ref[h * PEER_TOPK:(h + 1) * PEER_TOPK, :] = gate


def _peer_topk(sc, tl):
    t = sc.shape[-1]
    return pl.pallas_call(
        _peer_topk_kernel,
        grid=(t // tl,),
        in_specs=[pl.BlockSpec((2 * PEER_HEADS, PEER_N_KEYS, tl), lambda i: (0, 0, i))],
        out_specs=[pl.BlockSpec((PEER_SEL, tl), lambda i: (0, i)),
                   pl.BlockSpec((PEER_SEL, tl), lambda i: (0, i))],
        out_shape=[jax.ShapeDtypeStruct((PEER_SEL, t), jnp.int32),
                   jax.ShapeDtypeStruct((PEER_SEL, t), F32)],
        compiler_params=pltpu.CompilerParams(
            dimension_semantics=("parallel",), vmem_limit_bytes=VMEM_LIMIT),
        name="peer_topk",
    )(sc)


PEER_TOK_BLOCK = 128
PEER_TOK_GROUP = 8


def _peer_ffn_kernel(idx_hbm, gate_ref, xn_ref, h1_ref, uv_hbm, out_ref,
                     idx_smem, buf, idx_sem, sem):
    rows_per_group = PEER_TOK_GROUP * PEER_SEL
    n_groups = PEER_TOK_BLOCK // PEER_TOK_GROUP
    d = xn_ref.shape[1]
    step = pl.program_id(0)

    idx_copy = pltpu.make_async_copy(
        idx_hbm.at[pl.ds(step * PEER_TOK_BLOCK * PEER_SEL, PEER_TOK_BLOCK * PEER_SEL)],
        idx_smem, idx_sem)
    idx_copy.start()
    idx_copy.wait()

    def issue(g, slot):
        def body(r, carry):
            e = idx_smem[g * rows_per_group + r]
            pltpu.make_async_copy(uv_hbm.at[pl.ds(e, 1)], buf.at[slot, pl.ds(r, 1)],
                                  sem.at[slot]).start()
            return carry
        lax.fori_loop(0, rows_per_group, body, 0)

    def wait(slot):
        pltpu.make_async_copy(uv_hbm.at[pl.ds(0, rows_per_group)], buf.at[slot],
                              sem.at[slot]).wait()

    lane = lax.broadcasted_iota(jnp.int32, (PEER_SEL, PEER_TOK_BLOCK), 1)

    def compute(g, slot):
        tok0 = pl.multiple_of(g * PEER_TOK_GROUP, PEER_TOK_GROUP)
        x_g = xn_ref[pl.ds(tok0, PEER_TOK_GROUP), :]
        coef = []
        for t in range(PEER_TOK_GROUP):
            r0 = t * PEER_SEL
            acc = jnp.zeros((PEER_SEL, LANES), F32)
            for c in range(d // LANES):
                cs = slice(c * LANES, (c + 1) * LANES)
                acc = acc + buf[slot, r0:r0 + PEER_SEL, cs] * x_g[t:t + 1, cs]
            hid = jnp.sum(acc, axis=-1, keepdims=True)
            gate = jnp.sum(jnp.where(lane == tok0 + t, gate_ref[...], 0.0),
                           axis=-1, keepdims=True)
            coef.append(gate * jax.nn.gelu(hid))
        for c in range(d // LANES):
            cs = slice(c * LANES, (c + 1) * LANES)
            o = jnp.concatenate(
                [jnp.sum(coef[t] * buf[slot, t * PEER_SEL:(t + 1) * PEER_SEL,
                                       d + c * LANES:d + (c + 1) * LANES], axis=0, keepdims=True)
                 for t in range(PEER_TOK_GROUP)], axis=0)
            out_ref[pl.ds(tok0, PEER_TOK_GROUP), cs] = h1_ref[pl.ds(tok0, PEER_TOK_GROUP), cs] + o

    issue(0, 0)

    def group_body(g, carry):
        slot = g % 2

        @pl.when(g + 1 < n_groups)
        def _():
            issue(g + 1, 1 - slot)

        wait(slot)
        compute(g, slot)
        return carry

    lax.fori_loop(0, n_groups, group_body, 0)


def _peer_ffn(idx_flat, gate_t, xn, h1, uv):
    t, d = xn.shape
    rows_per_group = PEER_TOK_GROUP * PEER_SEL
    return pl.pallas_call(
        _peer_ffn_kernel,
        grid=(t // PEER_TOK_BLOCK,),
        in_specs=[pl.BlockSpec(memory_space=pl.ANY),
                  pl.BlockSpec((PEER_SEL, PEER_TOK_BLOCK), lambda i: (0, i)),
                  pl.BlockSpec((PEER_TOK_BLOCK, d), lambda i: (i, 0)),
                  pl.BlockSpec((PEER_TOK_BLOCK, d), lambda i: (i, 0)),
                  pl.BlockSpec(memory_space=pl.ANY)],
        out_specs=pl.BlockSpec((PEER_TOK_BLOCK, d), lambda i: (i, 0)),
        out_shape=jax.ShapeDtypeStruct((t, d), F32),
        scratch_shapes=[pltpu.SMEM((PEER_TOK_BLOCK * PEER_SEL,), jnp.int32),
                        pltpu.VMEM((2, rows_per_group, 2 * d), F32),
                        pltpu.SemaphoreType.DMA(()),
                        pltpu.SemaphoreType.DMA((2,))],
        compiler_params=pltpu.CompilerParams(
            dimension_semantics=("arbitrary",), vmem_limit_bytes=VMEM_LIMIT),
        name="peer_ffn",
    )(idx_flat, gate_t, xn, h1, uv)


def _ple_out_kernel(h_ref, p_ref, g_ref, wg_ref, wp_ref, gf_ref, o_ref, *, final_norm):
    h = h_ref[...]
    hn = _rms(h, g_ref[...]).astype(BF16)
    gate = jax.nn.sigmoid(jnp.dot(hn, wg_ref[...], preferred_element_type=F32))
    proj = jnp.dot(p_ref[...].astype(BF16), wp_ref[...], preferred_element_type=F32)
    h = h + gate * proj
    o_ref[...] = _rms(h, gf_ref[...]) if final_norm else h


def _ple_out(h2, p2, g, wg, wp, gf, tm, final_norm):
    t, d = h2.shape
    return pl.pallas_call(
        functools.partial(_ple_out_kernel, final_norm=final_norm),
        grid=(t // tm,),
        in_specs=[pl.BlockSpec((tm, d), lambda i: (i, 0)),
                  pl.BlockSpec((tm, D_PLE), lambda i: (i, 0)),
                  _const_spec(g.shape), _const_spec(wg.shape), _const_spec(wp.shape),
                  _const_spec(gf.shape)],
        out_specs=pl.BlockSpec((tm, d), lambda i: (i, 0)),
        out_shape=jax.ShapeDtypeStruct((t, d), F32),
        compiler_params=pltpu.CompilerParams(
            dimension_semantics=("parallel",), vmem_limit_bytes=VMEM_LIMIT),
        name="ple_out",
    )(h2, p2, g, wg, wp, gf)


def _rope_weight_cols(w, n_heads, stride, offset):
    half = QK_ROPE_DIM // 2
    x1 = jnp.stack([w[:, h * stride + offset:h * stride + offset + half] for h in range(n_heads)], 1)
    x2 = jnp.stack([w[:, h * stride + offset + half:h * stride + offset + 2 * half]
                    for h in range(n_heads)], 1)
    k = w.shape[0]
    x1 = x1.reshape(k, n_heads // 2, 2 * half)
    x2 = x2.reshape(k, n_heads // 2, 2 * half)
    wa = jnp.concatenate([x1, x2], axis=-1).reshape(k, -1)
    wb = jnp.concatenate([x2, x1], axis=-1).reshape(k, -1)
    return wa, wb


def kernel(x, p, positions, mix_norm_g, w_in, pool_w, pool_scale, q_norm_g, w_uq, kv_norm_g,
           w_ukv, w_o, ffn_norm_g, peer_wq, peer_sub_keys, peer_u, peer_v, ple_norm_g,
           ple_w_gate, ple_w_proj, final_norm_g):
    b, s, d = x.shape
    t = b * s
    depth = w_in.shape[0]
    half = QK_ROPE_DIM // 2

    inv_freq = ROPE_THETA ** (-jnp.arange(0, QK_ROPE_DIM, 2, dtype=F32) / QK_ROPE_DIM)
    ang = positions.astype(F32)[..., None] * inv_freq
    cos, sin = jnp.cos(ang), jnp.sin(ang)
    cos4 = jnp.concatenate([cos, cos, cos, cos], axis=-1)
    sin4 = jnp.concatenate([-sin, -sin, sin, sin], axis=-1)

    tm_mix = min(256, s)
    tq = min(512, s)
    tm = min(256, t)
    tl = min(128, t)
    row = lambda a: a.reshape(1, -1)

    h = x
    for i in range(depth):
        o3 = D_POOL + 2 * LORA
        wk1 = w_in[i][:, o3:o3 + half]
        wk2 = w_in[i][:, o3 + half:o3 + 2 * half]
        w_in_ext = jnp.concatenate(
            [w_in[i][:, :o3], wk1, wk1, wk2, wk2, wk2, wk2, wk1, wk1], axis=1).astype(BF16)
        wqn = jnp.concatenate(
            [w_uq[i][:, hh * QK_HEAD_DIM:hh * QK_HEAD_DIM + QK_NOPE_DIM] for hh in range(N_HEADS)],
            axis=1).astype(BF16)
        wqa, wqb = _rope_weight_cols(w_uq[i], N_HEADS, QK_HEAD_DIM, QK_NOPE_DIM)
        kv_w = w_ukv[i].reshape(LORA, N_HEADS, QK_NOPE_DIM + V_HEAD_DIM)
        wk = kv_w[:, :, :QK_NOPE_DIM].reshape(LORA, -1).astype(BF16)
        wv = kv_w[:, :, QK_NOPE_DIM:].reshape(LORA, -1).astype(BF16)

        ypool, q, k, v = _mix_in(
            h, cos4, sin4, row(mix_norm_g[i]), w_in_ext, pool_w[i].astype(BF16),
            row(pool_scale[i]), row(q_norm_g[i]), wqn, wqa.astype(BF16), wqb.astype(BF16),
            row(kv_norm_g[i]), wk, wv, tm_mix)
        yattn = _attention(q, k, v, tq)

        wo = w_o[i].astype(BF16)
        sk = peer_sub_keys[i].reshape(2 * PEER_HEADS, PEER_N_KEYS, PEER_HALF).astype(BF16)
        h1, xn, sc = _peer_q(h.reshape(t, d), ypool.reshape(t, D_POOL), yattn.reshape(t, D_ATTN),
                             wo[:D_POOL], wo[D_POOL:], row(ffn_norm_g[i]),
                             peer_wq[i].astype(BF16), sk, tm)
        idx_t, gate_t = _peer_topk(sc, tl)
        idx_flat = idx_t.T.reshape(-1)
        uv = jnp.concatenate([peer_u[i], peer_v[i]], axis=1)
        h2 = _peer_ffn(idx_flat, gate_t, xn, h1, uv)

        out = _ple_out(h2, p[i].reshape(t, D_PLE), row(ple_norm_g[i]), ple_w_gate[i].astype(BF16),
                       ple_w_proj[i].astype(BF16), row(final_norm_g), min(512, t), i + 1 == depth)
        h = out.reshape(b, s, d)
    return h
```

```python
import functools

import jax
import jax.numpy as jnp
from jax import lax
from jax.experimental import pallas as pl
from jax.experimental.pallas import tpu as pltpu

D_MODEL = 2048
D_PLE = 256
RMS_EPS = 1e-6
POOL_WINDOWS = (2, 4, 8, 16)
POOL_GROUP_DIM = 128
D_POOL = 512
POOL_HALO = 16
N_HEADS = 12
QK_NOPE_DIM = 128
QK_ROPE_DIM = 64
QK_HEAD_DIM = QK_NOPE_DIM + QK_ROPE_DIM
V_HEAD_DIM = 128
LORA = 512
D_ATTN = N_HEADS * V_HEAD_DIM
ROPE_THETA = 10000.0
QK_PAD = 256
PEER_HEADS = 8
PEER_N_KEYS = 128
PEER_TOPK = 16
PEER_HALF = 128
PEER_SEL = PEER_HEADS * PEER_TOPK

LANES = 128
SUBLANES = 8
VMEM_LIMIT = 56 * 1024 * 1024

F32 = jnp.float32
BF16 = jnp.bfloat16
NEG = -1e30


def _rms(x, g):
    return x * lax.rsqrt(jnp.mean(x * x, axis=-1, keepdims=True) + RMS_EPS) * g


def _const_spec(shape):
    nd = len(shape)
    return pl.BlockSpec(shape, lambda *_: (0,) * nd, pipeline_mode=pl.Buffered(1))


def _mix_in_kernel(x_ref, cos_ref, sin_ref, g_ref, w_in_ref, pool_w_ref, pool_s_ref,
                   qg_ref, wqn_ref, wqa_ref, wqb_ref, kvg_ref, wk_ref, wv_ref,
                   ypool_ref, q_ref, k_ref, v_ref, halo_ref):
    tm = x_ref.shape[0]
    si = pl.program_id(1)
    xn = _rms(x_ref[...], g_ref[...]).astype(BF16)
    z = jnp.dot(xn, w_in_ref[...], preferred_element_type=F32)
    zp = z[:, :D_POOL]
    cosb = cos_ref[...]
    sinb = sin_ref[...]

    @pl.when(si == 0)
    def _():
        halo_ref[0:POOL_HALO, :] = jnp.zeros((POOL_HALO, D_POOL), F32)

    halo_ref[POOL_HALO:POOL_HALO + tm, :] = zp
    pos = si * tm + lax.broadcasted_iota(jnp.int32, (tm, 1), 0)
    for gi, w in enumerate(POOL_WINDOWS):
        c0 = gi * POOL_GROUP_DIM
        cur = zp[:, c0:c0 + POOL_GROUP_DIM]
        acc = cur
        for j in range(1, w):
            acc = acc + halo_ref[POOL_HALO - j:POOL_HALO - j + tm, c0:c0 + POOL_GROUP_DIM]
        cnt = jnp.minimum(pos + 1, w).astype(F32)
        pooled = (acc / cnt - cur).astype(BF16)
        mixed = jnp.dot(pooled, pool_w_ref[gi], preferred_element_type=F32)
        ypool_ref[:, c0:c0 + POOL_GROUP_DIM] = (
            mixed * pool_s_ref[:, c0:c0 + POOL_GROUP_DIM]).astype(BF16)
    halo_ref[0:POOL_HALO, :] = halo_ref[tm:tm + POOL_HALO, :]

    qn = _rms(z[:, D_POOL:D_POOL + LORA], qg_ref[...]).astype(BF16)
    q_nope = jnp.dot(qn, wqn_ref[...], preferred_element_type=F32)
    q_a = jnp.dot(qn, wqa_ref[...], preferred_element_type=F32)
    q_b = jnp.dot(qn, wqb_ref[...], preferred_element_type=F32)
    for h in range(N_HEADS):
        q_ref[:, h * QK_PAD:h * QK_PAD + QK_NOPE_DIM] = (
            q_nope[:, h * QK_NOPE_DIM:(h + 1) * QK_NOPE_DIM].astype(BF16))
    for j in range(N_HEADS // 2):
        rp = (q_a[:, j * LANES:(j + 1) * LANES] * cosb
              + q_b[:, j * LANES:(j + 1) * LANES] * sinb).astype(BF16)
        for h in (2 * j, 2 * j + 1):
            q_ref[:, h * QK_PAD + QK_NOPE_DIM:(h + 1) * QK_PAD] = rp

    kvn = _rms(z[:, D_POOL + LORA:D_POOL + 2 * LORA], kvg_ref[...]).astype(BF16)
    k_nope = jnp.dot(kvn, wk_ref[...], preferred_element_type=F32)
    v = jnp.dot(kvn, wv_ref[...], preferred_element_type=F32)
    v_ref[...] = v.astype(BF16)
    o = D_POOL + 2 * LORA
    kr = z[:, o:o + LANES] * cosb + z[:, o + LANES:o + 2 * LANES] * sinb
    lane = lax.broadcasted_iota(jnp.int32, (tm, LANES), 1)
    even = (lane // (QK_ROPE_DIM // 2)) % 2 == 0
    kr_even = jnp.where(even, kr, 0.0).astype(BF16)
    kr_odd = jnp.where(even, 0.0, kr).astype(BF16)
    for h in range(N_HEADS):
        k_ref[:, h * QK_PAD:h * QK_PAD + QK_NOPE_DIM] = (
            k_nope[:, h * QK_NOPE_DIM:(h + 1) * QK_NOPE_DIM].astype(BF16))
        k_ref[:, h * QK_PAD + QK_NOPE_DIM:(h + 1) * QK_PAD] = kr_even if h % 2 == 0 else kr_odd


def _mix_in(x, cos4, sin4, g, w_in_ext, pool_w, pool_s, qg, wqn, wqa, wqb, kvg, wk, wv, tm):
    b, s, d = x.shape
    tok = lambda width: pl.BlockSpec((None, tm, width), lambda bi, si: (bi, si, 0))
    return pl.pallas_call(
        _mix_in_kernel,
        grid=(b, s // tm),
        in_specs=[tok(d), tok(LANES), tok(LANES), _const_spec(g.shape), _const_spec(w_in_ext.shape),
                  _const_spec(pool_w.shape), _const_spec(pool_s.shape), _const_spec(qg.shape),
                  _const_spec(wqn.shape), _const_spec(wqa.shape), _const_spec(wqb.shape),
                  _const_spec(kvg.shape), _const_spec(wk.shape), _const_spec(wv.shape)],
        out_specs=[tok(D_POOL), tok(N_HEADS * QK_PAD), tok(N_HEADS * QK_PAD), tok(D_ATTN)],
        out_shape=[jax.ShapeDtypeStruct((b, s, D_POOL), BF16),
                   jax.ShapeDtypeStruct((b, s, N_HEADS * QK_PAD), BF16),
                   jax.ShapeDtypeStruct((b, s, N_HEADS * QK_PAD), BF16),
                   jax.ShapeDtypeStruct((b, s, D_ATTN), BF16)],
        scratch_shapes=[pltpu.VMEM((POOL_HALO + tm, D_POOL), F32)],
        compiler_params=pltpu.CompilerParams(
            dimension_semantics=("parallel", "arbitrary"), vmem_limit_bytes=VMEM_LIMIT),
        name="mix_in",
    )(x, cos4, sin4, g, w_in_ext, pool_w, pool_s, qg, wqn, wqa, wqb, kvg, wk, wv)


def _attn_kernel(q_ref, k_ref, v_ref, o_ref, m_ref, l_ref, acc_ref):
    tq = q_ref.shape[0]
    tk = k_ref.shape[0]
    qi = pl.program_id(1)
    kj = pl.program_id(2)
    scale = QK_HEAD_DIM ** -0.5

    @pl.when(kj == 0)
    def _():
        m_ref[...] = jnp.full(m_ref.shape, NEG, F32)
        l_ref[...] = jnp.zeros(l_ref.shape, F32)
        acc_ref[...] = jnp.zeros(acc_ref.shape, F32)

    def step(masked):
        if masked:
            row = lax.broadcasted_iota(jnp.int32, (tq, tk), 0)
            col = lax.broadcasted_iota(jnp.int32, (tq, tk), 1)
            keep = col <= row
        for h in range(N_HEADS):
            q = q_ref[:, h * QK_PAD:(h + 1) * QK_PAD]
            k = k_ref[:, h * QK_PAD:(h + 1) * QK_PAD]
            s = lax.dot_general(q, k, (((1,), (1,)), ((), ())),
                                preferred_element_type=F32) * scale
            if masked:
                s = jnp.where(keep, s, NEG)
            chunks = [s[:, c * LANES:(c + 1) * LANES] for c in range(tk // LANES)]
            m_part = functools.reduce(jnp.maximum, chunks)
            m_prev = m_ref[h]
            m_new = jnp.maximum(m_prev, jnp.max(m_part, axis=-1, keepdims=True))
            p = [jnp.exp(c - m_new) for c in chunks]
            alpha = jnp.exp(m_prev - m_new)
            l_ref[h] = alpha * l_ref[h] + functools.reduce(lambda a, b: a + b, p)
            hs = slice(h * V_HEAD_DIM, (h + 1) * V_HEAD_DIM)
            acc_ref[:, hs] = alpha * acc_ref[:, hs] + jnp.dot(
                jnp.concatenate(p, axis=1).astype(BF16), v_ref[:, hs],
                preferred_element_type=F32)
            m_ref[h] = m_new

    @pl.when(kj < qi)
    def _():
        step(False)

    @pl.when(kj == qi)
    def _():
        step(True)
        for h in range(N_HEADS):
            hs = slice(h * V_HEAD_DIM, (h + 1) * V_HEAD_DIM)
            l = jnp.sum(l_ref[h], axis=-1, keepdims=True)
            o_ref[:, hs] = (acc_ref[:, hs] / l).astype(BF16)


def _attention(q, k, v, tq):
    b, s, _ = q.shape
    n = s // tq
    kv_map = lambda bi, qi, kj: (bi, jnp.minimum(kj, qi), 0)
    return pl.pallas_call(
        _attn_kernel,
        grid=(b, n, n),
        in_specs=[pl.BlockSpec((None, tq, N_HEADS * QK_PAD), lambda bi, qi, kj: (bi, qi, 0)),
                  pl.BlockSpec((None, tq, N_HEADS * QK_PAD), kv_map),
                  pl.BlockSpec((None, tq, D_ATTN), kv_map)],
        out_specs=pl.BlockSpec((None, tq, D_ATTN), lambda bi, qi, kj: (bi, qi, 0)),
        out_shape=jax.ShapeDtypeStruct((b, s, D_ATTN), BF16),
        scratch_shapes=[pltpu.VMEM((N_HEADS, tq, LANES), F32), pltpu.VMEM((N_HEADS, tq, LANES), F32),
                        pltpu.VMEM((tq, D_ATTN), F32)],
        compiler_params=pltpu.CompilerParams(
            dimension_semantics=("parallel", "parallel", "arbitrary"),
            vmem_limit_bytes=VMEM_LIMIT),
        name="attention",
    )(q, k, v)


def _peer_q_kernel(x_ref, yp_ref, ya_ref, wop_ref, woa_ref, g_ref, wq_ref, sk_ref,
                   h1_ref, xn_ref, sc_ref):
    h1 = (x_ref[...]
          + jnp.dot(yp_ref[...], wop_ref[...], preferred_element_type=F32)
          + jnp.dot(ya_ref[...], woa_ref[...], preferred_element_type=F32))
    h1_ref[...] = h1
    xn = _rms(h1, g_ref[...])
    xn_ref[...] = xn
    q = jnp.dot(xn.astype(BF16), wq_ref[...], preferred_element_type=F32).astype(BF16)
    for hp in range(2 * PEER_HEADS):
        sc_ref[hp] = lax.dot_general(sk_ref[hp], q[:, hp * PEER_HALF:(hp + 1) * PEER_HALF],
                                     (((1,), (1,)), ((), ())), preferred_element_type=F32)


def _peer_q(x2, ypool2, yattn2, wop, woa, g, wq, sk, tm):
    t, d = x2.shape
    row = lambda width: pl.BlockSpec((tm, width), lambda i: (i, 0))
    return pl.pallas_call(
        _peer_q_kernel,
        grid=(t // tm,),
        in_specs=[row(d), row(D_POOL), row(D_ATTN), _const_spec(wop.shape), _const_spec(woa.shape),
                  _const_spec(g.shape), _const_spec(wq.shape), _const_spec(sk.shape)],
        out_specs=[row(d), row(d),
                   pl.BlockSpec((2 * PEER_HEADS, PEER_N_KEYS, tm), lambda i: (0, 0, i))],
        out_shape=[jax.ShapeDtypeStruct((t, d), F32), jax.ShapeDtypeStruct((t, d), F32),
                   jax.ShapeDtypeStruct((2 * PEER_HEADS, PEER_N_KEYS, t), F32)],
        compiler_params=pltpu.CompilerParams(
            dimension_semantics=("parallel",), vmem_limit_bytes=VMEM_LIMIT),
        name="peer_q",
    )(x2, ypool2, yattn2, wop, woa, g, wq, sk)


def _extract_topk(vals, order, payload, k):
    out_v, out_p = [], []
    for _ in range(k):
        m = jnp.max(vals, axis=0, keepdims=True)
        pos = jnp.min(jnp.where(vals == m, order, jnp.int32(2 ** 30)), axis=0, keepdims=True)
        hit = order == pos
        out_v.append(m)
        out_p.append(pos if payload is None
                     else jnp.max(jnp.where(hit, payload, -1), axis=0, keepdims=True))
        vals = jnp.where(hit, -jnp.inf, vals)
    return jnp.concatenate(out_v, axis=0), jnp.concatenate(out_p, axis=0)


_CAND_A_SPLIT = 8


def _candidates(x0, x1, combine):
    blocks = [combine(x0[0:1], x1)]
    blocks += [combine(x0[a:a + 1], x1[0:SUBLANES]) for a in range(1, _CAND_A_SPLIT)]
    blocks.append(combine(x0[_CAND_A_SPLIT:], x1[0:1]))
    return jnp.concatenate(blocks, axis=0)


def _peer_topk_kernel(sc_ref, idx_ref, gate_ref):
    tl = sc_ref.shape[-1]
    key_ids = lax.broadcasted_iota(jnp.int32, (PEER_N_KEYS, tl), 0)
    n_cand = PEER_TOPK + SUBLANES * (_CAND_A_SPLIT - 1) + (PEER_TOPK - _CAND_A_SPLIT)
    r = lax.broadcasted_iota(jnp.int32, (n_cand, tl), 0)
    mid = r - PEER_TOPK
    tail0 = PEER_TOPK + SUBLANES * (_CAND_A_SPLIT - 1)
    flat_pos = jnp.where(
        r < PEER_TOPK, r,
        jnp.where(r < tail0, (1 + mid // SUBLANES) * PEER_TOPK + mid % SUBLANES,
                  (r - tail0 + _CAND_A_SPLIT) * PEER_TOPK))
    for h in range(PEER_HEADS):
        v0, i0 = _extract_topk(sc_ref[2 * h], key_ids, None, PEER_TOPK)
        v1, i1 = _extract_topk(sc_ref[2 * h + 1], key_ids, None, PEER_TOPK)
        cand_v = _candidates(v0, v1, lambda x, y: x + y)
        cand_i = _candidates(i0, i1, lambda x, y: x * PEER_N_KEYS + y)
        best_v, best_i = _extract_topk(cand_v, flat_pos, cand_i, PEER_TOPK)
        e = jnp.exp(best_v - best_v[0:1])
        gate = e / jnp.sum(e, axis=0, keepdims=True)
        idx_ref[h * PEER_TOPK:(h + 1) * PEER_TOPK, :] = best_i
        gate_ref[h * PEER_TOPK:(h + 1) * PEER_TOPK, :] = gate


def _peer_topk(sc, tl):
    t = sc.shape[-1]
    return pl.pallas_call(
        _peer_topk_kernel,
        grid=(t // tl,),
        in_specs=[pl.BlockSpec((2 * PEER_HEADS, PEER_N_KEYS, tl), lambda i: (0, 0, i))],
        out_specs=[pl.BlockSpec((PEER_SEL, tl), lambda i: (0, i)),
                   pl.BlockSpec((PEER_SEL, tl), lambda i: (0, i))],
        out_shape=[jax.ShapeDtypeStruct((PEER_SEL, t), jnp.int32),
                   jax.ShapeDtypeStruct((PEER_SEL, t), F32)],
        compiler_params=pltpu.CompilerParams(
            dimension_semantics=("parallel",), vmem_limit_bytes=VMEM_LIMIT),
        name="peer_topk",
    )(sc)


PEER_TOK_BLOCK = 128
PEER_TOK_GROUP = 8


PEER_IDX_BLOCK = PEER_TOK_BLOCK * PEER_SEL
PEER_ROW_GROUPS = PEER_SEL // SUBLANES


def _peer_ffn_kernel(idx_hbm, gate_ref, xn_ref, h1_ref, uv_hbm, out_ref,
                     idx_smem, buf, idx_sem, sem):
    n_groups = PEER_TOK_BLOCK // PEER_TOK_GROUP
    rows_per_group = PEER_TOK_GROUP * PEER_SEL
    d = xn_ref.shape[1]
    n_col = d // LANES
    slab = 2 * n_col
    step = pl.program_id(0)
    last_step = pl.num_programs(0) - 1
    islot = step % 2

    def idx_copy(s, sl):
        return pltpu.make_async_copy(
            idx_hbm.at[pl.ds(s * PEER_IDX_BLOCK, PEER_IDX_BLOCK)],
            idx_smem.at[pl.ds(pl.multiple_of(sl * PEER_IDX_BLOCK, PEER_IDX_BLOCK), PEER_IDX_BLOCK)],
            idx_sem.at[sl])

    def issue_token(ibase, t, slot):
        for j in range(PEER_SEL):
            e = idx_smem[ibase + t * PEER_SEL + j]
            src = uv_hbm.at[pl.ds(pl.multiple_of(e * slab, slab), slab), :]
            dst = buf.at[slot, t * PEER_ROW_GROUPS + j // SUBLANES, :, j % SUBLANES, :]
            pltpu.make_async_copy(src, dst, sem.at[slot]).start()

    def wait(slot):
        pltpu.make_async_copy(buf.at[slot], buf.at[slot], sem.at[slot]).wait()

    lane = lax.broadcasted_iota(jnp.int32, (PEER_SEL, PEER_TOK_BLOCK), 1)

    def rows(slot, t, col):
        blk = buf[slot, t * PEER_ROW_GROUPS:(t + 1) * PEER_ROW_GROUPS, col]
        return blk.reshape(PEER_SEL, LANES)

    def group(g, slot, nxt_ibase):
        wait(slot)
        tok0 = pl.multiple_of(g * PEER_TOK_GROUP, PEER_TOK_GROUP)
        x_g = xn_ref[pl.ds(tok0, PEER_TOK_GROUP), :]
        coef = []
        for t in range(PEER_TOK_GROUP):
            issue_token(nxt_ibase, t, 1 - slot)
            acc = jnp.zeros((PEER_SEL, LANES), F32)
            for c in range(n_col):
                acc = acc + rows(slot, t, c) * x_g[t:t + 1, c * LANES:(c + 1) * LANES]
            hid = jnp.sum(acc, axis=-1, keepdims=True)
            gate = jnp.sum(jnp.where(lane == tok0 + t, gate_ref[...], 0.0),
                           axis=-1, keepdims=True)
            coef.append(gate * jax.nn.gelu(hid))
        for c in range(n_col):
            cs = slice(c * LANES, (c + 1) * LANES)
            o = jnp.concatenate(
                [jnp.sum(coef[t] * rows(slot, t, n_col + c), axis=0, keepdims=True)
                 for t in range(PEER_TOK_GROUP)], axis=0)
            out_ref[pl.ds(tok0, PEER_TOK_GROUP), cs] = h1_ref[pl.ds(tok0, PEER_TOK_GROUP), cs] + o

    @pl.when(step == 0)
    def _():
        first = idx_copy(0, 0)
        first.start()
        first.wait()
        for t in range(PEER_TOK_GROUP):
            issue_token(0, t, 0)

    nxt_step = jnp.minimum(step + 1, last_step)
    idx_copy(nxt_step, 1 - islot).start()
    cur_base = islot * PEER_IDX_BLOCK
    nxt_base = (1 - islot) * PEER_IDX_BLOCK

    def pair_body(i, carry):
        is_last = i == n_groups // 2 - 1

        @pl.when(is_last)
        def _():
            idx_copy(nxt_step, 1 - islot).wait()

        group(2 * i, 0, cur_base + (2 * i + 1) * rows_per_group)
        group(2 * i + 1, 1,
              jnp.where(is_last, nxt_base, cur_base + (2 * i + 2) * rows_per_group))
        return carry

    lax.fori_loop(0, n_groups // 2, pair_body, 0)

    @pl.when(step == last_step)
    def _():
        wait(0)


def _peer_ffn(idx_flat, gate_t, xn, h1, uv):
    t, d = xn.shape
    return pl.pallas_call(
        _peer_ffn_kernel,
        grid=(t // PEER_TOK_BLOCK,),
        in_specs=[pl.BlockSpec(memory_space=pl.ANY),
                  pl.BlockSpec((PEER_SEL, PEER_TOK_BLOCK), lambda i: (0, i)),
                  pl.BlockSpec((PEER_TOK_BLOCK, d), lambda i: (i, 0)),
                  pl.BlockSpec((PEER_TOK_BLOCK, d), lambda i: (i, 0)),
                  pl.BlockSpec(memory_space=pl.ANY)],
        out_specs=pl.BlockSpec((PEER_TOK_BLOCK, d), lambda i: (i, 0)),
        out_shape=jax.ShapeDtypeStruct((t, d), F32),
        scratch_shapes=[pltpu.SMEM((2 * PEER_IDX_BLOCK,), jnp.int32),
                        pltpu.VMEM((2, PEER_TOK_GROUP * PEER_ROW_GROUPS, 2 * d // LANES,
                                    SUBLANES, LANES), F32),
                        pltpu.SemaphoreType.DMA((2,)),
                        pltpu.SemaphoreType.DMA((2,))],
        compiler_params=pltpu.CompilerParams(
            dimension_semantics=("arbitrary",), vmem_limit_bytes=VMEM_LIMIT),
        name="peer_ffn",
    )(idx_flat, gate_t, xn, h1, uv)


def _ple_out_kernel(h_ref, p_ref, g_ref, wg_ref, wp_ref, gf_ref, o_ref, *, final_norm):
    h = h_ref[...]
    hn = _rms(h, g_ref[...]).astype(BF16)
    gate = jax.nn.sigmoid(jnp.dot(hn, wg_ref[...], preferred_element_type=F32))
    proj = jnp.dot(p_ref[...].astype(BF16), wp_ref[...], preferred_element_type=F32)
    h = h + gate * proj
    o_ref[...] = _rms(h, gf_ref[...]) if final_norm else h


def _ple_out(h2, p2, g, wg, wp, gf, tm, final_norm):
    t, d = h2.shape
    return pl.pallas_call(
        functools.partial(_ple_out_kernel, final_norm=final_norm),
        grid=(t // tm,),
        in_specs=[pl.BlockSpec((tm, d), lambda i: (i, 0)),
                  pl.BlockSpec((tm, D_PLE), lambda i: (i, 0)),
                  _const_spec(g.shape), _const_spec(wg.shape), _const_spec(wp.shape),
                  _const_spec(gf.shape)],
        out_specs=pl.BlockSpec((tm, d), lambda i: (i, 0)),
        out_shape=jax.ShapeDtypeStruct((t, d), F32),
        compiler_params=pltpu.CompilerParams(
            dimension_semantics=("parallel",), vmem_limit_bytes=VMEM_LIMIT),
        name="ple_out",
    )(h2, p2, g, wg, wp, gf)


def _rope_weight_cols(w, n_heads, stride, offset):
    half = QK_ROPE_DIM // 2
    x1 = jnp.stack([w[:, h * stride + offset:h * stride + offset + half] for h in range(n_heads)], 1)
    x2 = jnp.stack([w[:, h * stride + offset + half:h * stride + offset + 2 * half]
                    for h in range(n_heads)], 1)
    k = w.shape[0]
    x1 = x1.reshape(k, n_heads // 2, 2 * half)
    x2 = x2.reshape(k, n_heads // 2, 2 * half)
    wa = jnp.concatenate([x1, x2], axis=-1).reshape(k, -1)
    wb = jnp.concatenate([x2, x1], axis=-1).reshape(k, -1)
    return wa, wb


def kernel(x, p, positions, mix_norm_g, w_in, pool_w, pool_scale, q_norm_g, w_uq, kv_norm_g,
           w_ukv, w_o, ffn_norm_g, peer_wq, peer_sub_keys, peer_u, peer_v, ple_norm_g,
           ple_w_gate, ple_w_proj, final_norm_g):
    b, s, d = x.shape
    t = b * s
    depth = w_in.shape[0]
    half = QK_ROPE_DIM // 2

    inv_freq = ROPE_THETA ** (-jnp.arange(0, QK_ROPE_DIM, 2, dtype=F32) / QK_ROPE_DIM)
    ang = positions.astype(F32)[..., None] * inv_freq
    cos, sin = jnp.cos(ang), jnp.sin(ang)
    cos4 = jnp.concatenate([cos, cos, cos, cos], axis=-1)
    sin4 = jnp.concatenate([-sin, -sin, sin, sin], axis=-1)

    tm_mix = min(256, s)
    tq = min(512, s)
    tm = min(256, t)
    tl = min(128, t)
    row = lambda a: a.reshape(1, -1)

    h = x
    for i in range(depth):
        o3 = D_POOL + 2 * LORA
        wk1 = w_in[i][:, o3:o3 + half]
        wk2 = w_in[i][:, o3 + half:o3 + 2 * half]
        w_in_ext = jnp.concatenate(
            [w_in[i][:, :o3], wk1, wk1, wk2, wk2, wk2, wk2, wk1, wk1], axis=1).astype(BF16)
        wqn = jnp.concatenate(
            [w_uq[i][:, hh * QK_HEAD_DIM:hh * QK_HEAD_DIM + QK_NOPE_DIM] for hh in range(N_HEADS)],
            axis=1).astype(BF16)
        wqa, wqb = _rope_weight_cols(w_uq[i], N_HEADS, QK_HEAD_DIM, QK_NOPE_DIM)
        kv_w = w_ukv[i].reshape(LORA, N_HEADS, QK_NOPE_DIM + V_HEAD_DIM)
        wk = kv_w[:, :, :QK_NOPE_DIM].reshape(LORA, -1).astype(BF16)
        wv = kv_w[:, :, QK_NOPE_DIM:].reshape(LORA, -1).astype(BF16)

        ypool, q, k, v = _mix_in(
            h, cos4, sin4, row(mix_norm_g[i]), w_in_ext, pool_w[i].astype(BF16),
            row(pool_scale[i]), row(q_norm_g[i]), wqn, wqa.astype(BF16), wqb.astype(BF16),
            row(kv_norm_g[i]), wk, wv, tm_mix)
        yattn = _attention(q, k, v, tq)

        wo = w_o[i].astype(BF16)
        sk = peer_sub_keys[i].reshape(2 * PEER_HEADS, PEER_N_KEYS, PEER_HALF).astype(BF16)
        h1, xn, sc = _peer_q(h.reshape(t, d), ypool.reshape(t, D_POOL), yattn.reshape(t, D_ATTN),
                             wo[:D_POOL], wo[D_POOL:], row(ffn_norm_g[i]),
                             peer_wq[i].astype(BF16), sk, tm)
        idx_t, gate_t = _peer_topk(sc, tl)
        idx_flat = idx_t.T.reshape(-1)
        uv = jnp.concatenate([peer_u[i], peer_v[i]], axis=1).reshape(-1, LANES)
        h2 = _peer_ffn(idx_flat, gate_t, xn, h1, uv)

        out = _ple_out(h2, p[i].reshape(t, D_PLE), row(ple_norm_g[i]), ple_w_gate[i].astype(BF16),
                       ple_w_proj[i].astype(BF16), row(final_norm_g), min(512, t), i + 1 == depth)
        h = out.reshape(b, s, d)
    return h
```

```python
import functools

import jax
import jax.numpy as jnp
from jax import lax
from jax.experimental import pallas as pl
from jax.experimental.pallas import tpu as pltpu

D_MODEL = 2048
D_PLE = 256
RMS_EPS = 1e-6
POOL_WINDOWS = (2, 4, 8, 16)
POOL_GROUP_DIM = 128
D_POOL = 512
POOL_HALO = 16
N_HEADS = 12
QK_NOPE_DIM = 128
QK_ROPE_DIM = 64
QK_HEAD_DIM = QK_NOPE_DIM + QK_ROPE_DIM
V_HEAD_DIM = 128
LORA = 512
D_ATTN = N_HEADS * V_HEAD_DIM
ROPE_THETA = 10000.0
QK_PAD = 256
PEER_HEADS = 8
PEER_N_KEYS = 128
PEER_TOPK = 16
PEER_HALF = 128
PEER_SEL = PEER_HEADS * PEER_TOPK

LANES = 128
SUBLANES = 8
VMEM_LIMIT = 56 * 1024 * 1024

F32 = jnp.float32
BF16 = jnp.bfloat16
NEG = -1e30


def _rms(x, g):
    return x * lax.rsqrt(jnp.mean(x * x, axis=-1, keepdims=True) + RMS_EPS) * g


def _const_spec(shape):
    nd = len(shape)
    return pl.BlockSpec(shape, lambda *_: (0,) * nd, pipeline_mode=pl.Buffered(1))


def _mix_in_kernel(x_ref, cos_ref, sin_ref, g_ref, w_in_ref, pool_w_ref, pool_s_ref,
                   qg_ref, wqn_ref, wqa_ref, wqb_ref, kvg_ref, wk_ref, wv_ref,
                   ypool_ref, q_ref, k_ref, v_ref, halo_ref):
    tm = x_ref.shape[0]
    si = pl.program_id(1)
    xn = _rms(x_ref[...], g_ref[...]).astype(BF16)
    z = jnp.dot(xn, w_in_ref[...], preferred_element_type=F32)
    zp = z[:, :D_POOL]
    cosb = cos_ref[...]
    sinb = sin_ref[...]

    @pl.when(si == 0)
    def _():
        halo_ref[0:POOL_HALO, :] = jnp.zeros((POOL_HALO, D_POOL), F32)

    halo_ref[POOL_HALO:POOL_HALO + tm, :] = zp
    pos = si * tm + lax.broadcasted_iota(jnp.int32, (tm, 1), 0)
    for gi, w in enumerate(POOL_WINDOWS):
        c0 = gi * POOL_GROUP_DIM
        cur = zp[:, c0:c0 + POOL_GROUP_DIM]
        acc = cur
        for j in range(1, w):
            acc = acc + halo_ref[POOL_HALO - j:POOL_HALO - j + tm, c0:c0 + POOL_GROUP_DIM]
        cnt = jnp.minimum(pos + 1, w).astype(F32)
        pooled = (acc / cnt - cur).astype(BF16)
        mixed = jnp.dot(pooled, pool_w_ref[gi], preferred_element_type=F32)
        ypool_ref[:, c0:c0 + POOL_GROUP_DIM] = (
            mixed * pool_s_ref[:, c0:c0 + POOL_GROUP_DIM]).astype(BF16)
    halo_ref[0:POOL_HALO, :] = halo_ref[tm:tm + POOL_HALO, :]

    qn = _rms(z[:, D_POOL:D_POOL + LORA], qg_ref[...]).astype(BF16)
    q_nope = jnp.dot(qn, wqn_ref[...], preferred_element_type=F32)
    q_a = jnp.dot(qn, wqa_ref[...], preferred_element_type=F32)
    q_b = jnp.dot(qn, wqb_ref[...], preferred_element_type=F32)
    for h in range(N_HEADS):
        q_ref[:, h * QK_PAD:h * QK_PAD + QK_NOPE_DIM] = (
            q_nope[:, h * QK_NOPE_DIM:(h + 1) * QK_NOPE_DIM].astype(BF16))
    for j in range(N_HEADS // 2):
        rp = (q_a[:, j * LANES:(j + 1) * LANES] * cosb
              + q_b[:, j * LANES:(j + 1) * LANES] * sinb).astype(BF16)
        for h in (2 * j, 2 * j + 1):
            q_ref[:, h * QK_PAD + QK_NOPE_DIM:(h + 1) * QK_PAD] = rp

    kvn = _rms(z[:, D_POOL + LORA:D_POOL + 2 * LORA], kvg_ref[...]).astype(BF16)
    k_nope = jnp.dot(kvn, wk_ref[...], preferred_element_type=F32)
    v = jnp.dot(kvn, wv_ref[...], preferred_element_type=F32)
    v_ref[...] = v.astype(BF16)
    o = D_POOL + 2 * LORA
    kr = z[:, o:o + LANES] * cosb + z[:, o + LANES:o + 2 * LANES] * sinb
    lane = lax.broadcasted_iota(jnp.int32, (tm, LANES), 1)
    even = (lane // (QK_ROPE_DIM // 2)) % 2 == 0
    kr_even = jnp.where(even, kr, 0.0).astype(BF16)
    kr_odd = jnp.where(even, 0.0, kr).astype(BF16)
    for h in range(N_HEADS):
        k_ref[:, h * QK_PAD:h * QK_PAD + QK_NOPE_DIM] = (
            k_nope[:, h * QK_NOPE_DIM:(h + 1) * QK_NOPE_DIM].astype(BF16))
        k_ref[:, h * QK_PAD + QK_NOPE_DIM:(h + 1) * QK_PAD] = kr_even if h % 2 == 0 else kr_odd


def _mix_in(x, cos4, sin4, g, w_in_ext, pool_w, pool_s, qg, wqn, wqa, wqb, kvg, wk, wv, tm):
    b, s, d = x.shape
    tok = lambda width: pl.BlockSpec((None, tm, width), lambda bi, si: (bi, si, 0))
    return pl.pallas_call(
        _mix_in_kernel,
        grid=(b, s // tm),
        in_specs=[tok(d), tok(LANES), tok(LANES), _const_spec(g.shape), _const_spec(w_in_ext.shape),
                  _const_spec(pool_w.shape), _const_spec(pool_s.shape), _const_spec(qg.shape),
                  _const_spec(wqn.shape), _const_spec(wqa.shape), _const_spec(wqb.shape),
                  _const_spec(kvg.shape), _const_spec(wk.shape), _const_spec(wv.shape)],
        out_specs=[tok(D_POOL), tok(N_HEADS * QK_PAD), tok(N_HEADS * QK_PAD), tok(D_ATTN)],
        out_shape=[jax.ShapeDtypeStruct((b, s, D_POOL), BF16),
                   jax.ShapeDtypeStruct((b, s, N_HEADS * QK_PAD), BF16),
                   jax.ShapeDtypeStruct((b, s, N_HEADS * QK_PAD), BF16),
                   jax.ShapeDtypeStruct((b, s, D_ATTN), BF16)],
        scratch_shapes=[pltpu.VMEM((POOL_HALO + tm, D_POOL), F32)],
        compiler_params=pltpu.CompilerParams(
            dimension_semantics=("parallel", "arbitrary"), vmem_limit_bytes=VMEM_LIMIT),
        name="mix_in",
    )(x, cos4, sin4, g, w_in_ext, pool_w, pool_s, qg, wqn, wqa, wqb, kvg, wk, wv)


def _attn_kernel(q_ref, k_ref, v_ref, o_ref, m_ref, l_ref, acc_ref):
    tq = q_ref.shape[0]
    tk = k_ref.shape[0]
    qi = pl.program_id(1)
    kj = pl.program_id(2)
    scale = QK_HEAD_DIM ** -0.5

    @pl.when(kj == 0)
    def _():
        m_ref[...] = jnp.full(m_ref.shape, NEG, F32)
        l_ref[...] = jnp.zeros(l_ref.shape, F32)
        acc_ref[...] = jnp.zeros(acc_ref.shape, F32)

    def step(masked):
        if masked:
            row = lax.broadcasted_iota(jnp.int32, (tq, tk), 0)
            col = lax.broadcasted_iota(jnp.int32, (tq, tk), 1)
            keep = col <= row
        for h in range(N_HEADS):
            q = q_ref[:, h * QK_PAD:(h + 1) * QK_PAD]
            k = k_ref[:, h * QK_PAD:(h + 1) * QK_PAD]
            s = lax.dot_general(q, k, (((1,), (1,)), ((), ())),
                                preferred_element_type=F32) * scale
            if masked:
                s = jnp.where(keep, s, NEG)
            chunks = [s[:, c * LANES:(c + 1) * LANES] for c in range(tk // LANES)]
            m_part = functools.reduce(jnp.maximum, chunks)
            m_prev = m_ref[h]
            m_new = jnp.maximum(m_prev, jnp.max(m_part, axis=-1, keepdims=True))
            p = [jnp.exp(c - m_new) for c in chunks]
            alpha = jnp.exp(m_prev - m_new)
            l_ref[h] = alpha * l_ref[h] + functools.reduce(lambda a, b: a + b, p)
            hs = slice(h * V_HEAD_DIM, (h + 1) * V_HEAD_DIM)
            acc_ref[:, hs] = alpha * acc_ref[:, hs] + jnp.dot(
                jnp.concatenate(p, axis=1).astype(BF16), v_ref[:, hs],
                preferred_element_type=F32)
            m_ref[h] = m_new

    @pl.when(kj < qi)
    def _():
        step(False)

    @pl.when(kj == qi)
    def _():
        step(True)
        for h in range(N_HEADS):
            hs = slice(h * V_HEAD_DIM, (h + 1) * V_HEAD_DIM)
            l = jnp.sum(l_ref[h], axis=-1, keepdims=True)
            o_ref[:, hs] = (acc_ref[:, hs] / l).astype(BF16)


def _attention(q, k, v, tq):
    b, s, _ = q.shape
    n = s // tq
    kv_map = lambda bi, qi, kj: (bi, jnp.minimum(kj, qi), 0)
    return pl.pallas_call(
        _attn_kernel,
        grid=(b, n, n),
        in_specs=[pl.BlockSpec((None, tq, N_HEADS * QK_PAD), lambda bi, qi, kj: (bi, qi, 0)),
                  pl.BlockSpec((None, tq, N_HEADS * QK_PAD), kv_map),
                  pl.BlockSpec((None, tq, D_ATTN), kv_map)],
        out_specs=pl.BlockSpec((None, tq, D_ATTN), lambda bi, qi, kj: (bi, qi, 0)),
        out_shape=jax.ShapeDtypeStruct((b, s, D_ATTN), BF16),
        scratch_shapes=[pltpu.VMEM((N_HEADS, tq, LANES), F32), pltpu.VMEM((N_HEADS, tq, LANES), F32),
                        pltpu.VMEM((tq, D_ATTN), F32)],
        compiler_params=pltpu.CompilerParams(
            dimension_semantics=("parallel", "parallel", "arbitrary"),
            vmem_limit_bytes=VMEM_LIMIT),
        name="attention",
    )(q, k, v)


def _peer_q_kernel(x_ref, yp_ref, ya_ref, wop_ref, woa_ref, g_ref, wq_ref, sk_ref,
                   h1_ref, xn_ref, sc_ref):
    h1 = (x_ref[...]
          + jnp.dot(yp_ref[...], wop_ref[...], preferred_element_type=F32)
          + jnp.dot(ya_ref[...], woa_ref[...], preferred_element_type=F32))
    h1_ref[...] = h1
    xn = _rms(h1, g_ref[...])
    xn_ref[...] = xn
    q = jnp.dot(xn.astype(BF16), wq_ref[...], preferred_element_type=F32).astype(BF16)
    for hp in range(2 * PEER_HEADS):
        sc_ref[hp] = lax.dot_general(sk_ref[hp], q[:, hp * PEER_HALF:(hp + 1) * PEER_HALF],
                                     (((1,), (1,)), ((), ())), preferred_element_type=F32)


def _peer_q(x2, ypool2, yattn2, wop, woa, g, wq, sk, tm):
    t, d = x2.shape
    row = lambda width: pl.BlockSpec((tm, width), lambda i: (i, 0))
    return pl.pallas_call(
        _peer_q_kernel,
        grid=(t // tm,),
        in_specs=[row(d), row(D_POOL), row(D_ATTN), _const_spec(wop.shape), _const_spec(woa.shape),
                  _const_spec(g.shape), _const_spec(wq.shape), _const_spec(sk.shape)],
        out_specs=[row(d), row(d),
                   pl.BlockSpec((2 * PEER_HEADS, PEER_N_KEYS, tm), lambda i: (0, 0, i))],
        out_shape=[jax.ShapeDtypeStruct((t, d), F32), jax.ShapeDtypeStruct((t, d), F32),
                   jax.ShapeDtypeStruct((2 * PEER_HEADS, PEER_N_KEYS, t), F32)],
        compiler_params=pltpu.CompilerParams(
            dimension_semantics=("parallel",), vmem_limit_bytes=VMEM_LIMIT),
        name="peer_q",
    )(x2, ypool2, yattn2, wop, woa, g, wq, sk)


def _extract_topk(vals, order, payload, k):
    out_v, out_p = [], []
    for _ in range(k):
        m = jnp.max(vals, axis=0, keepdims=True)
        pos = jnp.min(jnp.where(vals == m, order, jnp.int32(2 ** 30)), axis=0, keepdims=True)
        hit = order == pos
        out_v.append(m)
        out_p.append(pos if payload is None
                     else jnp.max(jnp.where(hit, payload, -1), axis=0, keepdims=True))
        vals = jnp.where(hit, -jnp.inf, vals)
    return jnp.concatenate(out_v, axis=0), jnp.concatenate(out_p, axis=0)


_CAND_A_SPLIT = 8


def _candidates(x0, x1, combine):
    blocks = [combine(x0[0:1], x1)]
    blocks += [combine(x0[a:a + 1], x1[0:SUBLANES]) for a in range(1, _CAND_A_SPLIT)]
    blocks.append(combine(x0[_CAND_A_SPLIT:], x1[0:1]))
    return jnp.concatenate(blocks, axis=0)


def _peer_topk_kernel(sc_ref, idx_ref, gate_ref):
    tl = sc_ref.shape[-1]
    key_ids = lax.broadcasted_iota(jnp.int32, (PEER_N_KEYS, tl), 0)
    n_cand = PEER_TOPK + SUBLANES * (_CAND_A_SPLIT - 1) + (PEER_TOPK - _CAND_A_SPLIT)
    r = lax.broadcasted_iota(jnp.int32, (n_cand, tl), 0)
    mid = r - PEER_TOPK
    tail0 = PEER_TOPK + SUBLANES * (_CAND_A_SPLIT - 1)
    flat_pos = jnp.where(
        r < PEER_TOPK, r,
        jnp.where(r < tail0, (1 + mid // SUBLANES) * PEER_TOPK + mid % SUBLANES,
                  (r - tail0 + _CAND_A_SPLIT) * PEER_TOPK))
    for h in range(PEER_HEADS):
        v0, i0 = _extract_topk(sc_ref[2 * h], key_ids, None, PEER_TOPK)
        v1, i1 = _extract_topk(sc_ref[2 * h + 1], key_ids, None, PEER_TOPK)
        cand_v = _candidates(v0, v1, lambda x, y: x + y)
        cand_i = _candidates(i0, i1, lambda x, y: x * PEER_N_KEYS + y)
        best_v, best_i = _extract_topk(cand_v, flat_pos, cand_i, PEER_TOPK)
        e = jnp.exp(best_v - best_v[0:1])
        gate = e / jnp.sum(e, axis=0, keepdims=True)
        idx_ref[h * PEER_TOPK:(h + 1) * PEER_TOPK, :] = best_i
        gate_ref[h * PEER_TOPK:(h + 1) * PEER_TOPK, :] = gate


def _peer_topk(sc, tl):
    t = sc.shape[-1]
    return pl.pallas_call(
        _peer_topk_kernel,
        grid=(t // tl,),
        in_specs=[pl.BlockSpec((2 * PEER_HEADS, PEER_N_KEYS, tl), lambda i: (0, 0, i))],
        out_specs=[pl.BlockSpec((PEER_SEL, tl), lambda i: (0, i)),
                   pl.BlockSpec((PEER_SEL, tl), lambda i: (0, i))],
        out_shape=[jax.ShapeDtypeStruct((PEER_SEL, t), jnp.int32),
                   jax.ShapeDtypeStruct((PEER_SEL, t), F32)],
        compiler_params=pltpu.CompilerParams(
            dimension_semantics=("parallel",), vmem_limit_bytes=VMEM_LIMIT),
        name="peer_topk",
    )(sc)


PEER_TOK_BLOCK = 128
PEER_TOK_GROUP = 8
PEER_SLOTS = 4
PEER_IDX_BLOCK = PEER_TOK_BLOCK * PEER_SEL
PEER_ROW_GROUPS = PEER_SEL // SUBLANES


def _pack_uv(u, v):
    ub = lax.bitcast_convert_type(u.astype(BF16), jnp.uint16).astype(jnp.uint32)
    vb = lax.bitcast_convert_type(v.astype(BF16), jnp.uint16).astype(jnp.uint32)
    return (ub | (vb << 16)).reshape(-1, LANES)


def _peer_ffn_kernel(idx_hbm, gate_ref, xn_ref, h1_ref, uv_hbm, out_ref,
                     idx_smem, *scratch):
    bufs, (idx_sem, sem) = scratch[:PEER_SLOTS], scratch[PEER_SLOTS:]
    n_groups = PEER_TOK_BLOCK // PEER_TOK_GROUP
    rows_per_group = PEER_TOK_GROUP * PEER_SEL
    dist = PEER_SLOTS - 1
    d = xn_ref.shape[1]
    n_col = d // LANES
    slab = n_col
    step = pl.program_id(0)
    last_step = pl.num_programs(0) - 1
    islot = step % 2

    def idx_copy(s, sl):
        return pltpu.make_async_copy(
            idx_hbm.at[pl.ds(s * PEER_IDX_BLOCK, PEER_IDX_BLOCK)],
            idx_smem.at[pl.ds(pl.multiple_of(sl * PEER_IDX_BLOCK, PEER_IDX_BLOCK), PEER_IDX_BLOCK)],
            idx_sem.at[sl])

    def issue_token(ibase, t, slot):
        for j in range(PEER_SEL):
            e = idx_smem[ibase + t * PEER_SEL + j]
            src = uv_hbm.at[pl.ds(pl.multiple_of(e * slab, slab), slab), :]
            dst = bufs[slot].at[t * PEER_ROW_GROUPS + j // SUBLANES, :, j % SUBLANES, :]
            pltpu.make_async_copy(src, dst, sem.at[slot]).start()

    def wait(slot):
        pltpu.make_async_copy(bufs[slot], bufs[slot], sem.at[slot]).wait()

    lane = lax.broadcasted_iota(jnp.int32, (PEER_SEL, PEER_TOK_BLOCK), 1)

    def words(slot, t, col):
        blk = bufs[slot][t * PEER_ROW_GROUPS:(t + 1) * PEER_ROW_GROUPS, col]
        return blk.reshape(PEER_SEL, LANES)

    def u_rows(w):
        return lax.bitcast_convert_type(w << 16, F32)

    def v_rows(w):
        return lax.bitcast_convert_type(w & jnp.uint32(0xFFFF0000), F32)

    def group(g, slot, nxt_ibase):
        wait(slot)
        tok0 = pl.multiple_of(g * PEER_TOK_GROUP, PEER_TOK_GROUP)
        x_g = xn_ref[pl.ds(tok0, PEER_TOK_GROUP), :]
        coef = []
        for t in range(PEER_TOK_GROUP):
            issue_token(nxt_ibase, t, (slot + dist) % PEER_SLOTS)
            acc = jnp.zeros((PEER_SEL, LANES), F32)
            for c in range(n_col):
                acc = acc + u_rows(words(slot, t, c)) * x_g[t:t + 1, c * LANES:(c + 1) * LANES]
            hid = jnp.sum(acc, axis=-1, keepdims=True)
            gate = jnp.sum(jnp.where(lane == tok0 + t, gate_ref[...], 0.0),
                           axis=-1, keepdims=True)
            coef.append(gate * jax.nn.gelu(hid))
        for c in range(n_col):
            cs = slice(c * LANES, (c + 1) * LANES)
            o = jnp.concatenate(
                [jnp.sum(coef[t] * v_rows(words(slot, t, c)), axis=0, keepdims=True)
                 for t in range(PEER_TOK_GROUP)], axis=0)
            out_ref[pl.ds(tok0, PEER_TOK_GROUP), cs] = h1_ref[pl.ds(tok0, PEER_TOK_GROUP), cs] + o

    @pl.when(step == 0)
    def _():
        first = idx_copy(0, 0)
        first.start()
        first.wait()

        for k in range(dist):
            def body(r, carry, k=k):
                e = idx_smem[k * rows_per_group + r]
                src = uv_hbm.at[pl.ds(pl.multiple_of(e * slab, slab), slab), :]
                dst = bufs[k].at[r // SUBLANES, :, r % SUBLANES, :]
                pltpu.make_async_copy(src, dst, sem.at[k]).start()
                return carry
            lax.fori_loop(0, rows_per_group, body, 0)

    nxt_step = jnp.minimum(step + 1, last_step)
    idx_copy(nxt_step, 1 - islot).start()
    cur_base = islot * PEER_IDX_BLOCK
    nxt_base = (1 - islot) * PEER_IDX_BLOCK
    n_rounds = n_groups // PEER_SLOTS

    def round_body(i, carry):
        is_last = i == n_rounds - 1

        @pl.when(is_last)
        def _():
            idx_copy(nxt_step, 1 - islot).wait()

        for k in range(PEER_SLOTS):
            g = PEER_SLOTS * i + k
            ahead = cur_base + (g + dist) * rows_per_group
            if k + dist >= PEER_SLOTS:
                ahead = jnp.where(is_last, nxt_base + (k + dist - PEER_SLOTS) * rows_per_group,
                                  ahead)
            group(g, k, ahead)
        return carry

    lax.fori_loop(0, n_rounds, round_body, 0)

    @pl.when(step == last_step)
    def _():
        for k in range(dist):
            wait(k)


def _peer_ffn(idx_flat, gate_t, xn, h1, uv):
    t, d = xn.shape
    return pl.pallas_call(
        _peer_ffn_kernel,
        grid=(t // PEER_TOK_BLOCK,),
        in_specs=[pl.BlockSpec(memory_space=pl.ANY),
                  pl.BlockSpec((PEER_SEL, PEER_TOK_BLOCK), lambda i: (0, i)),
                  pl.BlockSpec((PEER_TOK_BLOCK, d), lambda i: (i, 0)),
                  pl.BlockSpec((PEER_TOK_BLOCK, d), lambda i: (i, 0)),
                  pl.BlockSpec(memory_space=pl.ANY)],
        out_specs=pl.BlockSpec((PEER_TOK_BLOCK, d), lambda i: (i, 0)),
        out_shape=jax.ShapeDtypeStruct((t, d), F32),
        scratch_shapes=[pltpu.SMEM((2 * PEER_IDX_BLOCK,), jnp.int32),
                        *[pltpu.VMEM((PEER_TOK_GROUP * PEER_ROW_GROUPS, d // LANES,
                                      SUBLANES, LANES), jnp.uint32) for _ in range(PEER_SLOTS)],
                        pltpu.SemaphoreType.DMA((2,)),
                        pltpu.SemaphoreType.DMA((PEER_SLOTS,))],
        compiler_params=pltpu.CompilerParams(
            dimension_semantics=("arbitrary",), vmem_limit_bytes=VMEM_LIMIT),
        name="peer_ffn",
    )(idx_flat, gate_t, xn, h1, uv)


def _ple_out_kernel(h_ref, p_ref, g_ref, wg_ref, wp_ref, gf_ref, o_ref, *, final_norm):
    h = h_ref[...]
    hn = _rms(h, g_ref[...]).astype(BF16)
    gate = jax.nn.sigmoid(jnp.dot(hn, wg_ref[...], preferred_element_type=F32))
    proj = jnp.dot(p_ref[...].astype(BF16), wp_ref[...], preferred_element_type=F32)
    h = h + gate * proj
    o_ref[...] = _rms(h, gf_ref[...]) if final_norm else h


def _ple_out(h2, p2, g, wg, wp, gf, tm, final_norm):
    t, d = h2.shape
    return pl.pallas_call(
        functools.partial(_ple_out_kernel, final_norm=final_norm),
        grid=(t // tm,),
        in_specs=[pl.BlockSpec((tm, d), lambda i: (i, 0)),
                  pl.BlockSpec((tm, D_PLE), lambda i: (i, 0)),
                  _const_spec(g.shape), _const_spec(wg.shape), _const_spec(wp.shape),
                  _const_spec(gf.shape)],
        out_specs=pl.BlockSpec((tm, d), lambda i: (i, 0)),
        out_shape=jax.ShapeDtypeStruct((t, d), F32),
        compiler_params=pltpu.CompilerParams(
            dimension_semantics=("parallel",), vmem_limit_bytes=VMEM_LIMIT),
        name="ple_out",
    )(h2, p2, g, wg, wp, gf)


def _rope_weight_cols(w, n_heads, stride, offset):
    half = QK_ROPE_DIM // 2
    x1 = jnp.stack([w[:, h * stride + offset:h * stride + offset + half] for h in range(n_heads)], 1)
    x2 = jnp.stack([w[:, h * stride + offset + half:h * stride + offset + 2 * half]
                    for h in range(n_heads)], 1)
    k = w.shape[0]
    x1 = x1.reshape(k, n_heads // 2, 2 * half)
    x2 = x2.reshape(k, n_heads // 2, 2 * half)
    wa = jnp.concatenate([x1, x2], axis=-1).reshape(k, -1)
    wb = jnp.concatenate([x2, x1], axis=-1).reshape(k, -1)
    return wa, wb


def kernel(x, p, positions, mix_norm_g, w_in, pool_w, pool_scale, q_norm_g, w_uq, kv_norm_g,
           w_ukv, w_o, ffn_norm_g, peer_wq, peer_sub_keys, peer_u, peer_v, ple_norm_g,
           ple_w_gate, ple_w_proj, final_norm_g):
    b, s, d = x.shape
    t = b * s
    depth = w_in.shape[0]
    half = QK_ROPE_DIM // 2

    inv_freq = ROPE_THETA ** (-jnp.arange(0, QK_ROPE_DIM, 2, dtype=F32) / QK_ROPE_DIM)
    ang = positions.astype(F32)[..., None] * inv_freq
    cos, sin = jnp.cos(ang), jnp.sin(ang)
    cos4 = jnp.concatenate([cos, cos, cos, cos], axis=-1)
    sin4 = jnp.concatenate([-sin, -sin, sin, sin], axis=-1)

    tm_mix = min(256, s)
    tq = min(512, s)
    tm = min(256, t)
    tl = min(128, t)
    row = lambda a: a.reshape(1, -1)

    h = x
    for i in range(depth):
        o3 = D_POOL + 2 * LORA
        wk1 = w_in[i][:, o3:o3 + half]
        wk2 = w_in[i][:, o3 + half:o3 + 2 * half]
        w_in_ext = jnp.concatenate(
            [w_in[i][:, :o3], wk1, wk1, wk2, wk2, wk2, wk2, wk1, wk1], axis=1).astype(BF16)
        wqn = jnp.concatenate(
            [w_uq[i][:, hh * QK_HEAD_DIM:hh * QK_HEAD_DIM + QK_NOPE_DIM] for hh in range(N_HEADS)],
            axis=1).astype(BF16)
        wqa, wqb = _rope_weight_cols(w_uq[i], N_HEADS, QK_HEAD_DIM, QK_NOPE_DIM)
        kv_w = w_ukv[i].reshape(LORA, N_HEADS, QK_NOPE_DIM + V_HEAD_DIM)
        wk = kv_w[:, :, :QK_NOPE_DIM].reshape(LORA, -1).astype(BF16)
        wv = kv_w[:, :, QK_NOPE_DIM:].reshape(LORA, -1).astype(BF16)

        ypool, q, k, v = _mix_in(
            h, cos4, sin4, row(mix_norm_g[i]), w_in_ext, pool_w[i].astype(BF16),
            row(pool_scale[i]), row(q_norm_g[i]), wqn, wqa.astype(BF16), wqb.astype(BF16),
            row(kv_norm_g[i]), wk, wv, tm_mix)
        yattn = _attention(q, k, v, tq)

        wo = w_o[i].astype(BF16)
        sk = peer_sub_keys[i].reshape(2 * PEER_HEADS, PEER_N_KEYS, PEER_HALF).astype(BF16)
        h1, xn, sc = _peer_q(h.reshape(t, d), ypool.reshape(t, D_POOL), yattn.reshape(t, D_ATTN),
                             wo[:D_POOL], wo[D_POOL:], row(ffn_norm_g[i]),
                             peer_wq[i].astype(BF16), sk, tm)
        idx_t, gate_t = _peer_topk(sc, tl)
        idx_flat = idx_t.T.reshape(-1)
        uv = _pack_uv(peer_u[i], peer_v[i])
        h2 = _peer_ffn(idx_flat, gate_t, xn, h1, uv)

        out = _ple_out(h2, p[i].reshape(t, D_PLE), row(ple_norm_g[i]), ple_w_gate[i].astype(BF16),
                       ple_w_proj[i].astype(BF16), row(final_norm_g), min(512, t), i + 1 == depth)
        h = out.reshape(b, s, d)
    return h
```

```python
import functools

import jax
import jax.numpy as jnp
from jax import lax
from jax.experimental import pallas as pl
from jax.experimental.pallas import tpu as pltpu

D_MODEL = 2048
D_PLE = 256
RMS_EPS = 1e-6
POOL_WINDOWS = (2, 4, 8, 16)
POOL_GROUP_DIM = 128
D_POOL = 512
POOL_HALO = 16
N_HEADS = 12
QK_NOPE_DIM = 128
QK_ROPE_DIM = 64
QK_HEAD_DIM = QK_NOPE_DIM + QK_ROPE_DIM
V_HEAD_DIM = 128
LORA = 512
D_ATTN = N_HEADS * V_HEAD_DIM
ROPE_THETA = 10000.0
QK_PAD = 256
PEER_HEADS = 8
PEER_N_KEYS = 128
PEER_TOPK = 16
PEER_HALF = 128
PEER_SEL = PEER_HEADS * PEER_TOPK

LANES = 128
SUBLANES = 8
VMEM_LIMIT = 56 * 1024 * 1024

F32 = jnp.float32
BF16 = jnp.bfloat16
NEG = -1e30


def _rms(x, g):
    return x * lax.rsqrt(jnp.mean(x * x, axis=-1, keepdims=True) + RMS_EPS) * g


def _const_spec(shape):
    nd = len(shape)
    return pl.BlockSpec(shape, lambda *_: (0,) * nd, pipeline_mode=pl.Buffered(1))


def _mix_in_kernel(x_ref, cos_ref, sin_ref, g_ref, w_in_ref, pool_w_ref, pool_s_ref,
                   qg_ref, wqn_ref, wqa_ref, wqb_ref, kvg_ref, wk_ref, wv_ref,
                   ypool_ref, q_ref, k_ref, v_ref, halo_ref):
    tm = x_ref.shape[0]
    si = pl.program_id(1)
    xn = _rms(x_ref[...], g_ref[...]).astype(BF16)
    z = jnp.dot(xn, w_in_ref[...], preferred_element_type=F32)
    zp = z[:, :D_POOL]
    cosb = cos_ref[...]
    sinb = sin_ref[...]

    @pl.when(si == 0)
    def _():
        halo_ref[0:POOL_HALO, :] = jnp.zeros((POOL_HALO, D_POOL), F32)

    halo_ref[POOL_HALO:POOL_HALO + tm, :] = zp
    pos = si * tm + lax.broadcasted_iota(jnp.int32, (tm, 1), 0)
    for gi, w in enumerate(POOL_WINDOWS):
        c0 = gi * POOL_GROUP_DIM
        cur = zp[:, c0:c0 + POOL_GROUP_DIM]
        acc = cur
        for j in range(1, w):
            acc = acc + halo_ref[POOL_HALO - j:POOL_HALO - j + tm, c0:c0 + POOL_GROUP_DIM]
        cnt = jnp.minimum(pos + 1, w).astype(F32)
        pooled = (acc / cnt - cur).astype(BF16)
        mixed = jnp.dot(pooled, pool_w_ref[gi], preferred_element_type=F32)
        ypool_ref[:, c0:c0 + POOL_GROUP_DIM] = (
            mixed * pool_s_ref[:, c0:c0 + POOL_GROUP_DIM]).astype(BF16)
    halo_ref[0:POOL_HALO, :] = halo_ref[tm:tm + POOL_HALO, :]

    qn = _rms(z[:, D_POOL:D_POOL + LORA], qg_ref[...]).astype(BF16)
    q_nope = jnp.dot(qn, wqn_ref[...], preferred_element_type=F32)
    q_a = jnp.dot(qn, wqa_ref[...], preferred_element_type=F32)
    q_b = jnp.dot(qn, wqb_ref[...], preferred_element_type=F32)
    for h in range(N_HEADS):
        q_ref[:, h * QK_PAD:h * QK_PAD + QK_NOPE_DIM] = (
            q_nope[:, h * QK_NOPE_DIM:(h + 1) * QK_NOPE_DIM].astype(BF16))
    for j in range(N_HEADS // 2):
        rp = (q_a[:, j * LANES:(j + 1) * LANES] * cosb
              + q_b[:, j * LANES:(j + 1) * LANES] * sinb).astype(BF16)
        for h in (2 * j, 2 * j + 1):
            q_ref[:, h * QK_PAD + QK_NOPE_DIM:(h + 1) * QK_PAD] = rp

    kvn = _rms(z[:, D_POOL + LORA:D_POOL + 2 * LORA], kvg_ref[...]).astype(BF16)
    k_nope = jnp.dot(kvn, wk_ref[...], preferred_element_type=F32)
    v = jnp.dot(kvn, wv_ref[...], preferred_element_type=F32)
    v_ref[...] = v.astype(BF16)
    o = D_POOL + 2 * LORA
    kr = z[:, o:o + LANES] * cosb + z[:, o + LANES:o + 2 * LANES] * sinb
    lane = lax.broadcasted_iota(jnp.int32, (tm, LANES), 1)
    even = (lane // (QK_ROPE_DIM // 2)) % 2 == 0
    kr_even = jnp.where(even, kr, 0.0).astype(BF16)
    kr_odd = jnp.where(even, 0.0, kr).astype(BF16)
    for h in range(N_HEADS):
        k_ref[:, h * QK_PAD:h * QK_PAD + QK_NOPE_DIM] = (
            k_nope[:, h * QK_NOPE_DIM:(h + 1) * QK_NOPE_DIM].astype(BF16))
        k_ref[:, h * QK_PAD + QK_NOPE_DIM:(h + 1) * QK_PAD] = kr_even if h % 2 == 0 else kr_odd


def _mix_in(x, cos4, sin4, g, w_in_ext, pool_w, pool_s, qg, wqn, wqa, wqb, kvg, wk, wv, tm):
    b, s, d = x.shape
    tok = lambda width: pl.BlockSpec((None, tm, width), lambda bi, si: (bi, si, 0))
    return pl.pallas_call(
        _mix_in_kernel,
        grid=(b, s // tm),
        in_specs=[tok(d), tok(LANES), tok(LANES), _const_spec(g.shape), _const_spec(w_in_ext.shape),
                  _const_spec(pool_w.shape), _const_spec(pool_s.shape), _const_spec(qg.shape),
                  _const_spec(wqn.shape), _const_spec(wqa.shape), _const_spec(wqb.shape),
                  _const_spec(kvg.shape), _const_spec(wk.shape), _const_spec(wv.shape)],
        out_specs=[tok(D_POOL), tok(N_HEADS * QK_PAD), tok(N_HEADS * QK_PAD), tok(D_ATTN)],
        out_shape=[jax.ShapeDtypeStruct((b, s, D_POOL), BF16),
                   jax.ShapeDtypeStruct((b, s, N_HEADS * QK_PAD), BF16),
                   jax.ShapeDtypeStruct((b, s, N_HEADS * QK_PAD), BF16),
                   jax.ShapeDtypeStruct((b, s, D_ATTN), BF16)],
        scratch_shapes=[pltpu.VMEM((POOL_HALO + tm, D_POOL), F32)],
        compiler_params=pltpu.CompilerParams(
            dimension_semantics=("parallel", "arbitrary"), vmem_limit_bytes=VMEM_LIMIT),
        name="mix_in",
    )(x, cos4, sin4, g, w_in_ext, pool_w, pool_s, qg, wqn, wqa, wqb, kvg, wk, wv)


def _attn_kernel(q_ref, k_ref, v_ref, o_ref, m_ref, l_ref, acc_ref):
    tq = q_ref.shape[0]
    tk = k_ref.shape[0]
    qi = pl.program_id(1)
    kj = pl.program_id(2)
    q_start = qi * tq
    k_start = kj * tk
    exp_scale = QK_HEAD_DIM ** -0.5 * 1.4426950408889634

    @pl.when(kj == 0)
    def _():
        m_ref[...] = jnp.full(m_ref.shape, NEG, F32)
        l_ref[...] = jnp.zeros(l_ref.shape, F32)
        acc_ref[...] = jnp.zeros(acc_ref.shape, F32)

    def step(masked):
        if masked:
            row = q_start + lax.broadcasted_iota(jnp.int32, (tq, tk), 0)
            col = k_start + lax.broadcasted_iota(jnp.int32, (tq, tk), 1)
            keep = col <= row
        for h in range(N_HEADS):
            q = q_ref[:, h * QK_PAD:(h + 1) * QK_PAD]
            k = k_ref[:, h * QK_PAD:(h + 1) * QK_PAD]
            s = lax.dot_general(q, k, (((1,), (1,)), ((), ())), preferred_element_type=F32)
            if masked:
                s = jnp.where(keep, s, NEG)
            chunks = [s[:, c * LANES:(c + 1) * LANES] for c in range(tk // LANES)]
            m_part = functools.reduce(jnp.maximum, chunks)
            m_prev = m_ref[h]
            m_new = jnp.maximum(m_prev, jnp.max(m_part, axis=-1, keepdims=True))
            p = [jnp.exp2((c - m_new) * exp_scale) for c in chunks]
            alpha = jnp.exp2((m_prev - m_new) * exp_scale)
            l_ref[h] = alpha * l_ref[h] + functools.reduce(lambda a, b: a + b, p)
            hs = slice(h * V_HEAD_DIM, (h + 1) * V_HEAD_DIM)
            acc_ref[:, hs] = alpha * acc_ref[:, hs] + jnp.dot(
                jnp.concatenate(p, axis=1).astype(BF16), v_ref[:, hs],
                preferred_element_type=F32)
            m_ref[h] = m_new

    @pl.when(k_start + tk - 1 <= q_start)
    def _():
        step(False)

    @pl.when(jnp.logical_and(k_start + tk - 1 > q_start, k_start <= q_start + tq - 1))
    def _():
        step(True)

    @pl.when(kj == (q_start + tq - 1) // tk)
    def _():
        for h in range(N_HEADS):
            hs = slice(h * V_HEAD_DIM, (h + 1) * V_HEAD_DIM)
            l = jnp.sum(l_ref[h], axis=-1, keepdims=True)
            o_ref[:, hs] = (acc_ref[:, hs] / l).astype(BF16)


def _attention(q, k, v, tq, tk):
    b, s, _ = q.shape
    kv_map = lambda bi, qi, kj: (bi, jnp.minimum(kj, (qi * tq + tq - 1) // tk), 0)
    return pl.pallas_call(
        _attn_kernel,
        grid=(b, s // tq, s // tk),
        in_specs=[pl.BlockSpec((None, tq, N_HEADS * QK_PAD), lambda bi, qi, kj: (bi, qi, 0)),
                  pl.BlockSpec((None, tk, N_HEADS * QK_PAD), kv_map),
                  pl.BlockSpec((None, tk, D_ATTN), kv_map)],
        out_specs=pl.BlockSpec((None, tq, D_ATTN), lambda bi, qi, kj: (bi, qi, 0)),
        out_shape=jax.ShapeDtypeStruct((b, s, D_ATTN), BF16),
        scratch_shapes=[pltpu.VMEM((N_HEADS, tq, LANES), F32), pltpu.VMEM((N_HEADS, tq, LANES), F32),
                        pltpu.VMEM((tq, D_ATTN), F32)],
        compiler_params=pltpu.CompilerParams(
            dimension_semantics=("parallel", "parallel", "arbitrary"),
            vmem_limit_bytes=VMEM_LIMIT),
        name="attention",
    )(q, k, v)


def _peer_q_kernel(x_ref, yp_ref, ya_ref, wop_ref, woa_ref, g_ref, wq_ref, sk_ref,
                   h1_ref, xn_ref, sc_ref):
    h1 = (x_ref[...]
          + jnp.dot(yp_ref[...], wop_ref[...], preferred_element_type=F32)
          + jnp.dot(ya_ref[...], woa_ref[...], preferred_element_type=F32))
    h1_ref[...] = h1
    xn = _rms(h1, g_ref[...])
    xn_ref[...] = xn
    q = jnp.dot(xn.astype(BF16), wq_ref[...], preferred_element_type=F32).astype(BF16)
    for hp in range(2 * PEER_HEADS):
        sc_ref[hp] = lax.dot_general(sk_ref[hp], q[:, hp * PEER_HALF:(hp + 1) * PEER_HALF],
                                     (((1,), (1,)), ((), ())), preferred_element_type=F32)


def _peer_q(x2, ypool2, yattn2, wop, woa, g, wq, sk, tm):
    t, d = x2.shape
    row = lambda width: pl.BlockSpec((tm, width), lambda i: (i, 0))
    return pl.pallas_call(
        _peer_q_kernel,
        grid=(t // tm,),
        in_specs=[row(d), row(D_POOL), row(D_ATTN), _const_spec(wop.shape), _const_spec(woa.shape),
                  _const_spec(g.shape), _const_spec(wq.shape), _const_spec(sk.shape)],
        out_specs=[row(d), row(d),
                   pl.BlockSpec((2 * PEER_HEADS, PEER_N_KEYS, tm), lambda i: (0, 0, i))],
        out_shape=[jax.ShapeDtypeStruct((t, d), F32), jax.ShapeDtypeStruct((t, d), F32),
                   jax.ShapeDtypeStruct((2 * PEER_HEADS, PEER_N_KEYS, t), F32)],
        compiler_params=pltpu.CompilerParams(
            dimension_semantics=("parallel",), vmem_limit_bytes=VMEM_LIMIT),
        name="peer_q",
    )(x2, ypool2, yattn2, wop, woa, g, wq, sk)


def _extract_topk(vals, order, payload, k):
    out_v, out_p = [], []
    for _ in range(k):
        m = jnp.max(vals, axis=0, keepdims=True)
        pos = jnp.min(jnp.where(vals == m, order, jnp.int32(2 ** 30)), axis=0, keepdims=True)
        hit = order == pos
        out_v.append(m)
        out_p.append(pos if payload is None
                     else jnp.max(jnp.where(hit, payload, -1), axis=0, keepdims=True))
        vals = jnp.where(hit, -jnp.inf, vals)
    return jnp.concatenate(out_v, axis=0), jnp.concatenate(out_p, axis=0)


_CAND_A_SPLIT = 8


def _candidates(x0, x1, combine):
    blocks = [combine(x0[0:1], x1)]
    blocks += [combine(x0[a:a + 1], x1[0:SUBLANES]) for a in range(1, _CAND_A_SPLIT)]
    blocks.append(combine(x0[_CAND_A_SPLIT:], x1[0:1]))
    return jnp.concatenate(blocks, axis=0)


def _peer_topk_kernel(sc_ref, idx_ref, gate_ref):
    tl = sc_ref.shape[-1]
    key_ids = lax.broadcasted_iota(jnp.int32, (PEER_N_KEYS, tl), 0)
    n_cand = PEER_TOPK + SUBLANES * (_CAND_A_SPLIT - 1) + (PEER_TOPK - _CAND_A_SPLIT)
    r = lax.broadcasted_iota(jnp.int32, (n_cand, tl), 0)
    mid = r - PEER_TOPK
    tail0 = PEER_TOPK + SUBLANES * (_CAND_A_SPLIT - 1)
    flat_pos = jnp.where(
        r < PEER_TOPK, r,
        jnp.where(r < tail0, (1 + mid // SUBLANES) * PEER_TOPK + mid % SUBLANES,
                  (r - tail0 + _CAND_A_SPLIT) * PEER_TOPK))
    for h in range(PEER_HEADS):
        v0, i0 = _extract_topk(sc_ref[2 * h], key_ids, None, PEER_TOPK)
        v1, i1 = _extract_topk(sc_ref[2 * h + 1], key_ids, None, PEER_TOPK)
        cand_v = _candidates(v0, v1, lambda x, y: x + y)
        cand_i = _candidates(i0, i1, lambda x, y: x * PEER_N_KEYS + y)
        best_v, best_i = _extract_topk(cand_v, flat_pos, cand_i, PEER_TOPK)
        e = jnp.exp(best_v - best_v[0:1])
        gate = e / jnp.sum(e, axis=0, keepdims=True)
        idx_ref[h * PEER_TOPK:(h + 1) * PEER_TOPK, :] = best_i
        gate_ref[h * PEER_TOPK:(h + 1) * PEER_TOPK, :] = gate


def _peer_topk(sc, tl):
    t = sc.shape[-1]
    return pl.pallas_call(
        _peer_topk_kernel,
        grid=(t // tl,),
        in_specs=[pl.BlockSpec((2 * PEER_HEADS, PEER_N_KEYS, tl), lambda i: (0, 0, i))],
        out_specs=[pl.BlockSpec((PEER_SEL, tl), lambda i: (0, i)),
                   pl.BlockSpec((PEER_SEL, tl), lambda i: (0, i))],
        out_shape=[jax.ShapeDtypeStruct((PEER_SEL, t), jnp.int32),
                   jax.ShapeDtypeStruct((PEER_SEL, t), F32)],
        compiler_params=pltpu.CompilerParams(
            dimension_semantics=("parallel",), vmem_limit_bytes=VMEM_LIMIT),
        name="peer_topk",
    )(sc)


PEER_TOK_BLOCK = 128
PEER_TOK_GROUP = 8
PEER_SLOTS = 4
PEER_DMA_THREADS = 2
PEER_IDX_BLOCK = PEER_TOK_BLOCK * PEER_SEL
PEER_ROW_GROUPS = PEER_SEL // SUBLANES


def _pack_uv(u, v):
    ub = lax.bitcast_convert_type(u.astype(BF16), jnp.uint16).astype(jnp.uint32)
    vb = lax.bitcast_convert_type(v.astype(BF16), jnp.uint16).astype(jnp.uint32)
    return (ub | (vb << 16)).reshape(-1, LANES)


def _peer_ffn_kernel(idx_hbm, gate_ref, xn_ref, h1_ref, uv_hbm, out_ref,
                     idx_smem, *scratch):
    bufs, (stage, idx_sem, sem) = scratch[:PEER_SLOTS], scratch[PEER_SLOTS:]
    n_groups = PEER_TOK_BLOCK // PEER_TOK_GROUP
    rows_per_group = PEER_TOK_GROUP * PEER_SEL
    dist = PEER_SLOTS - 1
    d = xn_ref.shape[1]
    n_col = d // LANES
    slab = n_col
    step = pl.program_id(0)
    last_step = pl.num_programs(0) - 1
    islot = step % 2

    def idx_copy(s, sl):
        return pltpu.make_async_copy(
            idx_hbm.at[pl.ds(s * PEER_IDX_BLOCK, PEER_IDX_BLOCK)],
            idx_smem.at[pl.ds(pl.multiple_of(sl * PEER_IDX_BLOCK, PEER_IDX_BLOCK), PEER_IDX_BLOCK)],
            idx_sem.at[sl])

    def issue_token(ibase, t, slot, j0, j1):
        for j in range(j0, j1):
            e = idx_smem[ibase + t * PEER_SEL + j]
            src = uv_hbm.at[pl.ds(pl.multiple_of(e * slab, slab), slab), :]
            dst = bufs[slot].at[t * PEER_ROW_GROUPS + j // SUBLANES, :, j % SUBLANES, :]
            pltpu.make_async_copy(src, dst, sem.at[slot]).start(priority=j % PEER_DMA_THREADS)

    def wait(slot):
        pltpu.make_async_copy(bufs[slot], bufs[slot], sem.at[slot]).wait()

    lane = lax.broadcasted_iota(jnp.int32, (PEER_SEL, PEER_TOK_BLOCK), 1)

    def words(slot, t, col):
        blk = bufs[slot][t * PEER_ROW_GROUPS:(t + 1) * PEER_ROW_GROUPS, col]
        return blk.reshape(PEER_SEL, LANES)

    def u_rows(w):
        return lax.bitcast_convert_type(w << 16, F32)

    def v_rows(w):
        return lax.bitcast_convert_type(w & jnp.uint32(0xFFFF0000), F32)

    def group(g, slot, nxt_ibase):
        wait(slot)
        tok0 = pl.multiple_of(g * PEER_TOK_GROUP, PEER_TOK_GROUP)
        x_g = xn_ref[pl.ds(tok0, PEER_TOK_GROUP), :]
        nxt_slot = (slot + dist) % PEER_SLOTS
        half = PEER_SEL // 2
        for t in range(PEER_TOK_GROUP):
            issue_token(nxt_ibase, t, nxt_slot, 0, half)
            acc = jnp.zeros((PEER_SEL, LANES), F32)
            for c in range(n_col):
                acc = acc + u_rows(words(slot, t, c)) * x_g[t:t + 1, c * LANES:(c + 1) * LANES]
            hid = jnp.sum(acc, axis=-1, keepdims=True)
            gate = jnp.sum(jnp.where(lane == tok0 + t, gate_ref[...], 0.0),
                           axis=-1, keepdims=True)
            coef = gate * jax.nn.gelu(hid)
            issue_token(nxt_ibase, t, nxt_slot, half, PEER_SEL)
            for c in range(n_col):
                stage[t:t + 1, c * LANES:(c + 1) * LANES] = jnp.sum(
                    coef * v_rows(words(slot, t, c)), axis=0, keepdims=True)
        out_ref[pl.ds(tok0, PEER_TOK_GROUP), :] = (
            h1_ref[pl.ds(tok0, PEER_TOK_GROUP), :] + stage[...])

    @pl.when(step == 0)
    def _():
        first = idx_copy(0, 0)
        first.start()
        first.wait()

        for k in range(dist):
            def body(r, carry, k=k):
                e = idx_smem[k * rows_per_group + r]
                src = uv_hbm.at[pl.ds(pl.multiple_of(e * slab, slab), slab), :]
                dst = bufs[k].at[r // SUBLANES, :, r % SUBLANES, :]
                pltpu.make_async_copy(src, dst, sem.at[k]).start()
                return carry
            lax.fori_loop(0, rows_per_group, body, 0)

    nxt_step = jnp.minimum(step + 1, last_step)
    idx_copy(nxt_step, 1 - islot).start()
    cur_base = islot * PEER_IDX_BLOCK
    nxt_base = (1 - islot) * PEER_IDX_BLOCK
    n_rounds = n_groups // PEER_SLOTS

    def round_body(i, carry):
        is_last = i == n_rounds - 1

        @pl.when(is_last)
        def _():
            idx_copy(nxt_step, 1 - islot).wait()

        for k in range(PEER_SLOTS):
            g = PEER_SLOTS * i + k
            ahead = cur_base + (g + dist) * rows_per_group
            if k + dist >= PEER_SLOTS:
                ahead = jnp.where(is_last, nxt_base + (k + dist - PEER_SLOTS) * rows_per_group,
                                  ahead)
            group(g, k, ahead)
        return carry

    lax.fori_loop(0, n_rounds, round_body, 0)

    @pl.when(step == last_step)
    def _():
        for k in range(dist):
            wait(k)


def _peer_ffn(idx_flat, gate_t, xn, h1, uv):
    t, d = xn.shape
    return pl.pallas_call(
        _peer_ffn_kernel,
        grid=(t // PEER_TOK_BLOCK,),
        in_specs=[pl.BlockSpec(memory_space=pl.ANY),
                  pl.BlockSpec((PEER_SEL, PEER_TOK_BLOCK), lambda i: (0, i)),
                  pl.BlockSpec((PEER_TOK_BLOCK, d), lambda i: (i, 0)),
                  pl.BlockSpec((PEER_TOK_BLOCK, d), lambda i: (i, 0)),
                  pl.BlockSpec(memory_space=pl.ANY)],
        out_specs=pl.BlockSpec((PEER_TOK_BLOCK, d), lambda i: (i, 0)),
        out_shape=jax.ShapeDtypeStruct((t, d), F32),
        scratch_shapes=[pltpu.SMEM((2 * PEER_IDX_BLOCK,), jnp.int32),
                        *[pltpu.VMEM((PEER_TOK_GROUP * PEER_ROW_GROUPS, d // LANES,
                                      SUBLANES, LANES), jnp.uint32) for _ in range(PEER_SLOTS)],
                        pltpu.VMEM((PEER_TOK_GROUP, d), F32),
                        pltpu.SemaphoreType.DMA((2,)),
                        pltpu.SemaphoreType.DMA((PEER_SLOTS,))],
        compiler_params=pltpu.CompilerParams(
            dimension_semantics=("arbitrary",), vmem_limit_bytes=VMEM_LIMIT),
        name="peer_ffn",
    )(idx_flat, gate_t, xn, h1, uv)


def _ple_out_kernel(h_ref, p_ref, g_ref, wg_ref, wp_ref, gf_ref, o_ref, *, final_norm):
    h = h_ref[...]
    hn = _rms(h, g_ref[...]).astype(BF16)
    gate = jax.nn.sigmoid(jnp.dot(hn, wg_ref[...], preferred_element_type=F32))
    proj = jnp.dot(p_ref[...].astype(BF16), wp_ref[...], preferred_element_type=F32)
    h = h + gate * proj
    o_ref[...] = _rms(h, gf_ref[...]) if final_norm else h


def _ple_out(h2, p2, g, wg, wp, gf, tm, final_norm):
    t, d = h2.shape
    return pl.pallas_call(
        functools.partial(_ple_out_kernel, final_norm=final_norm),
        grid=(t // tm,),
        in_specs=[pl.BlockSpec((tm, d), lambda i: (i, 0)),
                  pl.BlockSpec((tm, D_PLE), lambda i: (i, 0)),
                  _const_spec(g.shape), _const_spec(wg.shape), _const_spec(wp.shape),
                  _const_spec(gf.shape)],
        out_specs=pl.BlockSpec((tm, d), lambda i: (i, 0)),
        out_shape=jax.ShapeDtypeStruct((t, d), F32),
        compiler_params=pltpu.CompilerParams(
            dimension_semantics=("parallel",), vmem_limit_bytes=VMEM_LIMIT),
        name="ple_out",
    )(h2, p2, g, wg, wp, gf)


def _rope_weight_cols(w, n_heads, stride, offset):
    half = QK_ROPE_DIM // 2
    x1 = jnp.stack([w[:, h * stride + offset:h * stride + offset + half] for h in range(n_heads)], 1)
    x2 = jnp.stack([w[:, h * stride + offset + half:h * stride + offset + 2 * half]
                    for h in range(n_heads)], 1)
    k = w.shape[0]
    x1 = x1.reshape(k, n_heads // 2, 2 * half)
    x2 = x2.reshape(k, n_heads // 2, 2 * half)
    wa = jnp.concatenate([x1, x2], axis=-1).reshape(k, -1)
    wb = jnp.concatenate([x2, x1], axis=-1).reshape(k, -1)
    return wa, wb


def kernel(x, p, positions, mix_norm_g, w_in, pool_w, pool_scale, q_norm_g, w_uq, kv_norm_g,
           w_ukv, w_o, ffn_norm_g, peer_wq, peer_sub_keys, peer_u, peer_v, ple_norm_g,
           ple_w_gate, ple_w_proj, final_norm_g):
    b, s, d = x.shape
    t = b * s
    depth = w_in.shape[0]
    half = QK_ROPE_DIM // 2

    inv_freq = ROPE_THETA ** (-jnp.arange(0, QK_ROPE_DIM, 2, dtype=F32) / QK_ROPE_DIM)
    ang = positions.astype(F32)[..., None] * inv_freq
    cos, sin = jnp.cos(ang), jnp.sin(ang)
    cos4 = jnp.concatenate([cos, cos, cos, cos], axis=-1)
    sin4 = jnp.concatenate([-sin, -sin, sin, sin], axis=-1)

    tm_mix = min(256, s)
    tq = min(1024, s)
    tk = min(512, s)
    tm = min(256, t)
    tl = min(128, t)
    row = lambda a: a.reshape(1, -1)

    h = x
    for i in range(depth):
        o3 = D_POOL + 2 * LORA
        wk1 = w_in[i][:, o3:o3 + half]
        wk2 = w_in[i][:, o3 + half:o3 + 2 * half]
        w_in_ext = jnp.concatenate(
            [w_in[i][:, :o3], wk1, wk1, wk2, wk2, wk2, wk2, wk1, wk1], axis=1).astype(BF16)
        wqn = jnp.concatenate(
            [w_uq[i][:, hh * QK_HEAD_DIM:hh * QK_HEAD_DIM + QK_NOPE_DIM] for hh in range(N_HEADS)],
            axis=1).astype(BF16)
        wqa, wqb = _rope_weight_cols(w_uq[i], N_HEADS, QK_HEAD_DIM, QK_NOPE_DIM)
        kv_w = w_ukv[i].reshape(LORA, N_HEADS, QK_NOPE_DIM + V_HEAD_DIM)
        wk = kv_w[:, :, :QK_NOPE_DIM].reshape(LORA, -1).astype(BF16)
        wv = kv_w[:, :, QK_NOPE_DIM:].reshape(LORA, -1).astype(BF16)

        ypool, q, k, v = _mix_in(
            h, cos4, sin4, row(mix_norm_g[i]), w_in_ext, pool_w[i].astype(BF16),
            row(pool_scale[i]), row(q_norm_g[i]), wqn, wqa.astype(BF16), wqb.astype(BF16),
            row(kv_norm_g[i]), wk, wv, tm_mix)
        yattn = _attention(q, k, v, tq, tk)

        wo = w_o[i].astype(BF16)
        sk = peer_sub_keys[i].reshape(2 * PEER_HEADS, PEER_N_KEYS, PEER_HALF).astype(BF16)
        h1, xn, sc = _peer_q(h.reshape(t, d), ypool.reshape(t, D_POOL), yattn.reshape(t, D_ATTN),
                             wo[:D_POOL], wo[D_POOL:], row(ffn_norm_g[i]),
                             peer_wq[i].astype(BF16), sk, tm)
        idx_t, gate_t = _peer_topk(sc, tl)
        idx_flat = idx_t.T.reshape(-1)
        uv = _pack_uv(peer_u[i], peer_v[i])
        h2 = _peer_ffn(idx_flat, gate_t, xn, h1, uv)

        out = _ple_out(h2, p[i].reshape(t, D_PLE), row(ple_norm_g[i]), ple_w_gate[i].astype(BF16),
                       ple_w_proj[i].astype(BF16), row(final_norm_g), min(512, t), i + 1 == depth)
        h = out.reshape(b, s, d)
    return h
```

```python
import functools

import jax
import jax.numpy as jnp
from jax import lax
from jax.experimental import pallas as pl
from jax.experimental.pallas import tpu as pltpu

D_MODEL = 2048
D_PLE = 256
RMS_EPS = 1e-6
POOL_WINDOWS = (2, 4, 8, 16)
POOL_GROUP_DIM = 128
D_POOL = 512
POOL_HALO = 16
N_HEADS = 12
QK_NOPE_DIM = 128
QK_ROPE_DIM = 64
QK_HEAD_DIM = QK_NOPE_DIM + QK_ROPE_DIM
V_HEAD_DIM = 128
LORA = 512
D_ATTN = N_HEADS * V_HEAD_DIM
ROPE_THETA = 10000.0
QK_PAD = 256
PEER_HEADS = 8
PEER_N_KEYS = 128
PEER_TOPK = 16
PEER_HALF = 128
PEER_SEL = PEER_HEADS * PEER_TOPK

LANES = 128
SUBLANES = 8
VMEM_LIMIT = 56 * 1024 * 1024

F32 = jnp.float32
BF16 = jnp.bfloat16
NEG = -1e30


def _rms(x, g):
    return x * lax.rsqrt(jnp.mean(x * x, axis=-1, keepdims=True) + RMS_EPS) * g


def _const_spec(shape):
    nd = len(shape)
    return pl.BlockSpec(shape, lambda *_: (0,) * nd, pipeline_mode=pl.Buffered(1))


def _mix_in_kernel(x_ref, cos_ref, sin_ref, g_ref, w_in_ref, pool_w_ref, pool_s_ref,
                   qg_ref, wqn_ref, wqa_ref, wqb_ref, kvg_ref, wk_ref, wv_ref,
                   ypool_ref, q_ref, k_ref, v_ref, halo_ref):
    tm = x_ref.shape[0]
    si = pl.program_id(1)
    xn = _rms(x_ref[...], g_ref[...]).astype(BF16)
    z = jnp.dot(xn, w_in_ref[...], preferred_element_type=F32)
    zp = z[:, :D_POOL]
    cosb = cos_ref[...]
    sinb = sin_ref[...]

    @pl.when(si == 0)
    def _():
        halo_ref[0:POOL_HALO, :] = jnp.zeros((POOL_HALO, D_POOL), F32)

    halo_ref[POOL_HALO:POOL_HALO + tm, :] = zp
    pos = si * tm + lax.broadcasted_iota(jnp.int32, (tm, 1), 0)
    for gi, w in enumerate(POOL_WINDOWS):
        c0 = gi * POOL_GROUP_DIM
        cur = zp[:, c0:c0 + POOL_GROUP_DIM]
        acc = cur
        for j in range(1, w):
            acc = acc + halo_ref[POOL_HALO - j:POOL_HALO - j + tm, c0:c0 + POOL_GROUP_DIM]
        cnt = jnp.minimum(pos + 1, w).astype(F32)
        pooled = (acc / cnt - cur).astype(BF16)
        mixed = jnp.dot(pooled, pool_w_ref[gi], preferred_element_type=F32)
        ypool_ref[:, c0:c0 + POOL_GROUP_DIM] = (
            mixed * pool_s_ref[:, c0:c0 + POOL_GROUP_DIM]).astype(BF16)
    halo_ref[0:POOL_HALO, :] = halo_ref[tm:tm + POOL_HALO, :]

    qn = _rms(z[:, D_POOL:D_POOL + LORA], qg_ref[...]).astype(BF16)
    q_nope = jnp.dot(qn, wqn_ref[...], preferred_element_type=F32)
    q_a = jnp.dot(qn, wqa_ref[...], preferred_element_type=F32)
    q_b = jnp.dot(qn, wqb_ref[...], preferred_element_type=F32)
    for h in range(N_HEADS):
        q_ref[:, h * QK_PAD:h * QK_PAD + QK_NOPE_DIM] = (
            q_nope[:, h * QK_NOPE_DIM:(h + 1) * QK_NOPE_DIM].astype(BF16))
    for j in range(N_HEADS // 2):
        rp = (q_a[:, j * LANES:(j + 1) * LANES] * cosb
              + q_b[:, j * LANES:(j + 1) * LANES] * sinb).astype(BF16)
        for h in (2 * j, 2 * j + 1):
            q_ref[:, h * QK_PAD + QK_NOPE_DIM:(h + 1) * QK_PAD] = rp

    kvn = _rms(z[:, D_POOL + LORA:D_POOL + 2 * LORA], kvg_ref[...]).astype(BF16)
    k_nope = jnp.dot(kvn, wk_ref[...], preferred_element_type=F32)
    v = jnp.dot(kvn, wv_ref[...], preferred_element_type=F32)
    v_ref[...] = v.astype(BF16)
    o = D_POOL + 2 * LORA
    kr = z[:, o:o + LANES] * cosb + z[:, o + LANES:o + 2 * LANES] * sinb
    lane = lax.broadcasted_iota(jnp.int32, (tm, LANES), 1)
    even = (lane // (QK_ROPE_DIM // 2)) % 2 == 0
    kr_even = jnp.where(even, kr, 0.0).astype(BF16)
    kr_odd = jnp.where(even, 0.0, kr).astype(BF16)
    for h in range(N_HEADS):
        k_ref[:, h * QK_PAD:h * QK_PAD + QK_NOPE_DIM] = (
            k_nope[:, h * QK_NOPE_DIM:(h + 1) * QK_NOPE_DIM].astype(BF16))
        k_ref[:, h * QK_PAD + QK_NOPE_DIM:(h + 1) * QK_PAD] = kr_even if h % 2 == 0 else kr_odd


def _mix_in(x, cos4, sin4, g, w_in_ext, pool_w, pool_s, qg, wqn, wqa, wqb, kvg, wk, wv, tm):
    b, s, d = x.shape
    tok = lambda width: pl.BlockSpec((None, tm, width), lambda bi, si: (bi, si, 0))
    return pl.pallas_call(
        _mix_in_kernel,
        grid=(b, s // tm),
        in_specs=[tok(d), tok(LANES), tok(LANES), _const_spec(g.shape), _const_spec(w_in_ext.shape),
                  _const_spec(pool_w.shape), _const_spec(pool_s.shape), _const_spec(qg.shape),
                  _const_spec(wqn.shape), _const_spec(wqa.shape), _const_spec(wqb.shape),
                  _const_spec(kvg.shape), _const_spec(wk.shape), _const_spec(wv.shape)],
        out_specs=[tok(D_POOL), tok(N_HEADS * QK_PAD), tok(N_HEADS * QK_PAD), tok(D_ATTN)],
        out_shape=[jax.ShapeDtypeStruct((b, s, D_POOL), BF16),
                   jax.ShapeDtypeStruct((b, s, N_HEADS * QK_PAD), BF16),
                   jax.ShapeDtypeStruct((b, s, N_HEADS * QK_PAD), BF16),
                   jax.ShapeDtypeStruct((b, s, D_ATTN), BF16)],
        scratch_shapes=[pltpu.VMEM((POOL_HALO + tm, D_POOL), F32)],
        compiler_params=pltpu.CompilerParams(
            dimension_semantics=("parallel", "arbitrary"), vmem_limit_bytes=VMEM_LIMIT),
        name="mix_in",
    )(x, cos4, sin4, g, w_in_ext, pool_w, pool_s, qg, wqn, wqa, wqb, kvg, wk, wv)


def _attn_kernel(q_ref, k_ref, v_ref, o_ref, m_ref, l_ref, acc_ref):
    tq = q_ref.shape[0]
    tk = k_ref.shape[0]
    qi = pl.program_id(1)
    kj = pl.program_id(2)
    q_start = qi * tq
    k_start = kj * tk
    exp_scale = QK_HEAD_DIM ** -0.5 * 1.4426950408889634

    @pl.when(kj == 0)
    def _():
        m_ref[...] = jnp.full(m_ref.shape, NEG, F32)
        l_ref[...] = jnp.zeros(l_ref.shape, F32)
        acc_ref[...] = jnp.zeros(acc_ref.shape, F32)

    def step(masked):
        if masked:
            row = q_start + lax.broadcasted_iota(jnp.int32, (tq, tk), 0)
            col = k_start + lax.broadcasted_iota(jnp.int32, (tq, tk), 1)
            keep = col <= row
        for h in range(N_HEADS):
            q = q_ref[:, h * QK_PAD:(h + 1) * QK_PAD]
            k = k_ref[:, h * QK_PAD:(h + 1) * QK_PAD]
            s = lax.dot_general(q, k, (((1,), (1,)), ((), ())), preferred_element_type=F32)
            if masked:
                s = jnp.where(keep, s, NEG)
            chunks = [s[:, c * LANES:(c + 1) * LANES] for c in range(tk // LANES)]
            m_part = functools.reduce(jnp.maximum, chunks)
            m_prev = m_ref[h]
            m_new = jnp.maximum(m_prev, jnp.max(m_part, axis=-1, keepdims=True))
            p = [jnp.exp2((c - m_new) * exp_scale) for c in chunks]
            alpha = jnp.exp2((m_prev - m_new) * exp_scale)
            l_ref[h] = alpha * l_ref[h] + functools.reduce(lambda a, b: a + b, p)
            hs = slice(h * V_HEAD_DIM, (h + 1) * V_HEAD_DIM)
            acc_ref[:, hs] = alpha * acc_ref[:, hs] + jnp.dot(
                jnp.concatenate(p, axis=1).astype(BF16), v_ref[:, hs],
                preferred_element_type=F32)
            m_ref[h] = m_new

    @pl.when(k_start + tk - 1 <= q_start)
    def _():
        step(False)

    @pl.when(jnp.logical_and(k_start + tk - 1 > q_start, k_start <= q_start + tq - 1))
    def _():
        step(True)

    @pl.when(kj == (q_start + tq - 1) // tk)
    def _():
        for h in range(N_HEADS):
            hs = slice(h * V_HEAD_DIM, (h + 1) * V_HEAD_DIM)
            l = jnp.sum(l_ref[h], axis=-1, keepdims=True)
            o_ref[:, hs] = (acc_ref[:, hs] / l).astype(BF16)


def _attention(q, k, v, tq, tk):
    b, s, _ = q.shape
    kv_map = lambda bi, qi, kj: (bi, jnp.minimum(kj, (qi * tq + tq - 1) // tk), 0)
    return pl.pallas_call(
        _attn_kernel,
        grid=(b, s // tq, s // tk),
        in_specs=[pl.BlockSpec((None, tq, N_HEADS * QK_PAD), lambda bi, qi, kj: (bi, qi, 0)),
                  pl.BlockSpec((None, tk, N_HEADS * QK_PAD), kv_map),
                  pl.BlockSpec((None, tk, D_ATTN), kv_map)],
        out_specs=pl.BlockSpec((None, tq, D_ATTN), lambda bi, qi, kj: (bi, qi, 0)),
        out_shape=jax.ShapeDtypeStruct((b, s, D_ATTN), BF16),
        scratch_shapes=[pltpu.VMEM((N_HEADS, tq, LANES), F32), pltpu.VMEM((N_HEADS, tq, LANES), F32),
                        pltpu.VMEM((tq, D_ATTN), F32)],
        compiler_params=pltpu.CompilerParams(
            dimension_semantics=("parallel", "parallel", "arbitrary"),
            vmem_limit_bytes=VMEM_LIMIT),
        name="attention",
    )(q, k, v)


def _peer_q_kernel(x_ref, yp_ref, ya_ref, wop_ref, woa_ref, g_ref, wq_ref, sk_ref,
                   h1_ref, xn_ref, sc_ref):
    h1 = (x_ref[...]
          + jnp.dot(yp_ref[...], wop_ref[...], preferred_element_type=F32)
          + jnp.dot(ya_ref[...], woa_ref[...], preferred_element_type=F32))
    h1_ref[...] = h1
    xn = _rms(h1, g_ref[...])
    xn_ref[...] = xn
    q = jnp.dot(xn.astype(BF16), wq_ref[...], preferred_element_type=F32).astype(BF16)
    for hp in range(2 * PEER_HEADS):
        sc_ref[hp] = lax.dot_general(sk_ref[hp], q[:, hp * PEER_HALF:(hp + 1) * PEER_HALF],
                                     (((1,), (1,)), ((), ())), preferred_element_type=F32)


def _peer_q(x2, ypool2, yattn2, wop, woa, g, wq, sk, tm):
    t, d = x2.shape
    row = lambda width: pl.BlockSpec((tm, width), lambda i: (i, 0))
    return pl.pallas_call(
        _peer_q_kernel,
        grid=(t // tm,),
        in_specs=[row(d), row(D_POOL), row(D_ATTN), _const_spec(wop.shape), _const_spec(woa.shape),
                  _const_spec(g.shape), _const_spec(wq.shape), _const_spec(sk.shape)],
        out_specs=[row(d), row(d),
                   pl.BlockSpec((2 * PEER_HEADS, PEER_N_KEYS, tm), lambda i: (0, 0, i))],
        out_shape=[jax.ShapeDtypeStruct((t, d), F32), jax.ShapeDtypeStruct((t, d), F32),
                   jax.ShapeDtypeStruct((2 * PEER_HEADS, PEER_N_KEYS, t), F32)],
        compiler_params=pltpu.CompilerParams(
            dimension_semantics=("parallel",), vmem_limit_bytes=VMEM_LIMIT),
        name="peer_q",
    )(x2, ypool2, yattn2, wop, woa, g, wq, sk)


def _extract_topk(vals, order, payload, k):
    out_v, out_p = [], []
    for _ in range(k):
        m = jnp.max(vals, axis=0, keepdims=True)
        pos = jnp.min(jnp.where(vals == m, order, jnp.int32(2 ** 30)), axis=0, keepdims=True)
        hit = order == pos
        out_v.append(m)
        out_p.append(pos if payload is None
                     else jnp.max(jnp.where(hit, payload, -1), axis=0, keepdims=True))
        vals = jnp.where(hit, -jnp.inf, vals)
    return jnp.concatenate(out_v, axis=0), jnp.concatenate(out_p, axis=0)


_CAND_A_SPLIT = 8


def _candidates(x0, x1, combine):
    blocks = [combine(x0[0:1], x1)]
    blocks += [combine(x0[a:a + 1], x1[0:SUBLANES]) for a in range(1, _CAND_A_SPLIT)]
    blocks.append(combine(x0[_CAND_A_SPLIT:], x1[0:1]))
    return jnp.concatenate(blocks, axis=0)


def _peer_topk_kernel(sc_ref, idx_ref, gate_ref):
    tl = sc_ref.shape[-1]
    key_ids = lax.broadcasted_iota(jnp.int32, (PEER_N_KEYS, tl), 0)
    n_cand = PEER_TOPK + SUBLANES * (_CAND_A_SPLIT - 1) + (PEER_TOPK - _CAND_A_SPLIT)
    r = lax.broadcasted_iota(jnp.int32, (n_cand, tl), 0)
    mid = r - PEER_TOPK
    tail0 = PEER_TOPK + SUBLANES * (_CAND_A_SPLIT - 1)
    flat_pos = jnp.where(
        r < PEER_TOPK, r,
        jnp.where(r < tail0, (1 + mid // SUBLANES) * PEER_TOPK + mid % SUBLANES,
                  (r - tail0 + _CAND_A_SPLIT) * PEER_TOPK))
    for h in range(PEER_HEADS):
        v0, i0 = _extract_topk(sc_ref[2 * h], key_ids, None, PEER_TOPK)
        v1, i1 = _extract_topk(sc_ref[2 * h + 1], key_ids, None, PEER_TOPK)
        cand_v = _candidates(v0, v1, lambda x, y: x + y)
        cand_i = _candidates(i0, i1, lambda x, y: x * PEER_N_KEYS + y)
        best_v, best_i = _extract_topk(cand_v, flat_pos, cand_i, PEER_TOPK)
        e = jnp.exp(best_v - best_v[0:1])
        gate = e / jnp.sum(e, axis=0, keepdims=True)
        idx_ref[h * PEER_TOPK:(h + 1) * PEER_TOPK, :] = best_i
        gate_ref[h * PEER_TOPK:(h + 1) * PEER_TOPK, :] = gate


def _peer_topk(sc, tl):
    t = sc.shape[-1]
    return pl.pallas_call(
        _peer_topk_kernel,
        grid=(t // tl,),
        in_specs=[pl.BlockSpec((2 * PEER_HEADS, PEER_N_KEYS, tl), lambda i: (0, 0, i))],
        out_specs=[pl.BlockSpec((PEER_SEL, tl), lambda i: (0, i)),
                   pl.BlockSpec((PEER_SEL, tl), lambda i: (0, i))],
        out_shape=[jax.ShapeDtypeStruct((PEER_SEL, t), jnp.int32),
                   jax.ShapeDtypeStruct((PEER_SEL, t), F32)],
        compiler_params=pltpu.CompilerParams(
            dimension_semantics=("parallel",), vmem_limit_bytes=VMEM_LIMIT),
        name="peer_topk",
    )(sc)


PEER_TOK_BLOCK = 128
PEER_TOK_GROUP = 8
PEER_SLOTS = 4
PEER_DMA_THREADS = 2
PEER_IDX_BLOCK = PEER_TOK_BLOCK * PEER_SEL
PEER_ROW_GROUPS = PEER_SEL // SUBLANES


def _pack_uv(u, v):
    ub = lax.bitcast_convert_type(u.astype(BF16), jnp.uint16).astype(jnp.uint32)
    vb = lax.bitcast_convert_type(v.astype(BF16), jnp.uint16).astype(jnp.uint32)
    return (ub | (vb << 16)).reshape(-1, LANES)


def _peer_ffn_kernel(idx_hbm, gate_ref, xn_ref, h1_ref, uv_hbm, p_ref, pg_ref, wg_ref, wp_ref,
                     gf_ref, out_ref, idx_smem, *scratch, final_norm):
    bufs, (stage, h2_scr, idx_sem, sem) = scratch[:PEER_SLOTS], scratch[PEER_SLOTS:]
    n_groups = PEER_TOK_BLOCK // PEER_TOK_GROUP
    rows_per_group = PEER_TOK_GROUP * PEER_SEL
    dist = PEER_SLOTS - 1
    d = xn_ref.shape[1]
    n_col = d // LANES
    slab = n_col
    step = pl.program_id(0)
    last_step = pl.num_programs(0) - 1
    islot = step % 2

    def idx_copy(s, sl):
        return pltpu.make_async_copy(
            idx_hbm.at[pl.ds(s * PEER_IDX_BLOCK, PEER_IDX_BLOCK)],
            idx_smem.at[pl.ds(pl.multiple_of(sl * PEER_IDX_BLOCK, PEER_IDX_BLOCK), PEER_IDX_BLOCK)],
            idx_sem.at[sl])

    def issue_token(ibase, t, slot, j0, j1):
        for j in range(j0, j1):
            e = idx_smem[ibase + t * PEER_SEL + j]
            src = uv_hbm.at[pl.ds(pl.multiple_of(e * slab, slab), slab), :]
            dst = bufs[slot].at[t * PEER_ROW_GROUPS + j // SUBLANES, :, j % SUBLANES, :]
            pltpu.make_async_copy(src, dst, sem.at[slot]).start(priority=j % PEER_DMA_THREADS)

    def wait(slot):
        pltpu.make_async_copy(bufs[slot], bufs[slot], sem.at[slot]).wait()

    lane = lax.broadcasted_iota(jnp.int32, (PEER_SEL, PEER_TOK_BLOCK), 1)

    def words(slot, t, col):
        blk = bufs[slot][t * PEER_ROW_GROUPS:(t + 1) * PEER_ROW_GROUPS, col]
        return blk.reshape(PEER_SEL, LANES)

    def u_rows(w):
        return lax.bitcast_convert_type(w << 16, F32)

    def v_rows(w):
        return lax.bitcast_convert_type(w & jnp.uint32(0xFFFF0000), F32)

    def group(g, slot, nxt_ibase):
        tok0 = pl.multiple_of(g * PEER_TOK_GROUP, PEER_TOK_GROUP)
        x_g = xn_ref[pl.ds(tok0, PEER_TOK_GROUP), :]
        nxt_slot = (slot + dist) % PEER_SLOTS
        half = PEER_SEL // 2
        per_col = half // n_col
        for t in range(PEER_TOK_GROUP):
            if t == PEER_TOK_GROUP - 1:
                wait((slot + 1) % PEER_SLOTS)
            acc = jnp.zeros((PEER_SEL, LANES), F32)
            for c in range(n_col):
                issue_token(nxt_ibase, t, nxt_slot, c * per_col, (c + 1) * per_col)
                acc = acc + u_rows(words(slot, t, c)) * x_g[t:t + 1, c * LANES:(c + 1) * LANES]
            hid = jnp.sum(acc, axis=-1, keepdims=True)
            gate = jnp.sum(jnp.where(lane == tok0 + t, gate_ref[...], 0.0),
                           axis=-1, keepdims=True)
            coef = gate * jax.nn.gelu(hid)
            for c in range(n_col):
                issue_token(nxt_ibase, t, nxt_slot, half + c * per_col, half + (c + 1) * per_col)
                stage[t:t + 1, c * LANES:(c + 1) * LANES] = jnp.sum(
                    coef * v_rows(words(slot, t, c)), axis=0, keepdims=True)
        h2_scr[pl.ds(tok0, PEER_TOK_GROUP), :] = (
            h1_ref[pl.ds(tok0, PEER_TOK_GROUP), :] + stage[...])

    @pl.when(step == 0)
    def _():
        first = idx_copy(0, 0)
        first.start()
        first.wait()

        for k in range(dist):
            def body(r, carry, k=k):
                e = idx_smem[k * rows_per_group + r]
                src = uv_hbm.at[pl.ds(pl.multiple_of(e * slab, slab), slab), :]
                dst = bufs[k].at[r // SUBLANES, :, r % SUBLANES, :]
                pltpu.make_async_copy(src, dst, sem.at[k]).start()
                return carry
            lax.fori_loop(0, rows_per_group, body, 0)
        wait(0)

    nxt_step = jnp.minimum(step + 1, last_step)
    idx_copy(nxt_step, 1 - islot).start()
    cur_base = islot * PEER_IDX_BLOCK
    nxt_base = (1 - islot) * PEER_IDX_BLOCK
    n_rounds = n_groups // PEER_SLOTS

    def round_body(i, carry):
        is_last = i == n_rounds - 1

        @pl.when(is_last)
        def _():
            idx_copy(nxt_step, 1 - islot).wait()

        for k in range(PEER_SLOTS):
            g = PEER_SLOTS * i + k
            ahead = cur_base + (g + dist) * rows_per_group
            if k + dist >= PEER_SLOTS:
                ahead = jnp.where(is_last, nxt_base + (k + dist - PEER_SLOTS) * rows_per_group,
                                  ahead)
            group(g, k, ahead)
        return carry

    lax.fori_loop(0, n_rounds, round_body, 0)

    h = h2_scr[...]
    hn = _rms(h, pg_ref[...]).astype(BF16)
    ple_gate = jax.nn.sigmoid(jnp.dot(hn, wg_ref[...], preferred_element_type=F32))
    proj = jnp.dot(p_ref[...].astype(BF16), wp_ref[...], preferred_element_type=F32)
    h = h + ple_gate * proj
    out_ref[...] = _rms(h, gf_ref[...]) if final_norm else h

    @pl.when(step == last_step)
    def _():
        for k in range(1, dist):
            wait(k)


def _peer_ffn(idx_flat, gate_t, xn, h1, uv, p2, pg, wg, wp, gf, final_norm):
    t, d = xn.shape
    tile = lambda width: pl.BlockSpec((PEER_TOK_BLOCK, width), lambda i: (i, 0))
    return pl.pallas_call(
        functools.partial(_peer_ffn_kernel, final_norm=final_norm),
        grid=(t // PEER_TOK_BLOCK,),
        in_specs=[pl.BlockSpec(memory_space=pl.ANY),
                  pl.BlockSpec((PEER_SEL, PEER_TOK_BLOCK), lambda i: (0, i)),
                  tile(d), tile(d),
                  pl.BlockSpec(memory_space=pl.ANY),
                  tile(D_PLE), _const_spec(pg.shape), _const_spec(wg.shape),
                  _const_spec(wp.shape), _const_spec(gf.shape)],
        out_specs=tile(d),
        out_shape=jax.ShapeDtypeStruct((t, d), F32),
        scratch_shapes=[pltpu.SMEM((2 * PEER_IDX_BLOCK,), jnp.int32),
                        *[pltpu.VMEM((PEER_TOK_GROUP * PEER_ROW_GROUPS, d // LANES,
                                      SUBLANES, LANES), jnp.uint32) for _ in range(PEER_SLOTS)],
                        pltpu.VMEM((PEER_TOK_GROUP, d), F32),
                        pltpu.VMEM((PEER_TOK_BLOCK, d), F32),
                        pltpu.SemaphoreType.DMA((2,)),
                        pltpu.SemaphoreType.DMA((PEER_SLOTS,))],
        compiler_params=pltpu.CompilerParams(
            dimension_semantics=("arbitrary",), vmem_limit_bytes=VMEM_LIMIT),
        name="peer_ffn",
    )(idx_flat, gate_t, xn, h1, uv, p2, pg, wg, wp, gf)


def _rope_weight_cols(w, n_heads, stride, offset):
    half = QK_ROPE_DIM // 2
    x1 = jnp.stack([w[:, h * stride + offset:h * stride + offset + half] for h in range(n_heads)], 1)
    x2 = jnp.stack([w[:, h * stride + offset + half:h * stride + offset + 2 * half]
                    for h in range(n_heads)], 1)
    k = w.shape[0]
    x1 = x1.reshape(k, n_heads // 2, 2 * half)
    x2 = x2.reshape(k, n_heads // 2, 2 * half)
    wa = jnp.concatenate([x1, x2], axis=-1).reshape(k, -1)
    wb = jnp.concatenate([x2, x1], axis=-1).reshape(k, -1)
    return wa, wb


def kernel(x, p, positions, mix_norm_g, w_in, pool_w, pool_scale, q_norm_g, w_uq, kv_norm_g,
           w_ukv, w_o, ffn_norm_g, peer_wq, peer_sub_keys, peer_u, peer_v, ple_norm_g,
           ple_w_gate, ple_w_proj, final_norm_g):
    b, s, d = x.shape
    t = b * s
    depth = w_in.shape[0]
    half = QK_ROPE_DIM // 2

    inv_freq = ROPE_THETA ** (-jnp.arange(0, QK_ROPE_DIM, 2, dtype=F32) / QK_ROPE_DIM)
    ang = positions.astype(F32)[..., None] * inv_freq
    cos, sin = jnp.cos(ang), jnp.sin(ang)
    cos4 = jnp.concatenate([cos, cos, cos, cos], axis=-1)
    sin4 = jnp.concatenate([-sin, -sin, sin, sin], axis=-1)

    tm_mix = min(256, s)
    tq = min(1024, s)
    tk = min(512, s)
    tm = min(256, t)
    tl = min(128, t)
    row = lambda a: a.reshape(1, -1)

    h = x
    for i in range(depth):
        o3 = D_POOL + 2 * LORA
        wk1 = w_in[i][:, o3:o3 + half]
        wk2 = w_in[i][:, o3 + half:o3 + 2 * half]
        w_in_ext = jnp.concatenate(
            [w_in[i][:, :o3], wk1, wk1, wk2, wk2, wk2, wk2, wk1, wk1], axis=1).astype(BF16)
        wqn = jnp.concatenate(
            [w_uq[i][:, hh * QK_HEAD_DIM:hh * QK_HEAD_DIM + QK_NOPE_DIM] for hh in range(N_HEADS)],
            axis=1).astype(BF16)
        wqa, wqb = _rope_weight_cols(w_uq[i], N_HEADS, QK_HEAD_DIM, QK_NOPE_DIM)
        kv_w = w_ukv[i].reshape(LORA, N_HEADS, QK_NOPE_DIM + V_HEAD_DIM)
        wk = kv_w[:, :, :QK_NOPE_DIM].reshape(LORA, -1).astype(BF16)
        wv = kv_w[:, :, QK_NOPE_DIM:].reshape(LORA, -1).astype(BF16)

        ypool, q, k, v = _mix_in(
            h, cos4, sin4, row(mix_norm_g[i]), w_in_ext, pool_w[i].astype(BF16),
            row(pool_scale[i]), row(q_norm_g[i]), wqn, wqa.astype(BF16), wqb.astype(BF16),
            row(kv_norm_g[i]), wk, wv, tm_mix)
        yattn = _attention(q, k, v, tq, tk)

        wo = w_o[i].astype(BF16)
        sk = peer_sub_keys[i].reshape(2 * PEER_HEADS, PEER_N_KEYS, PEER_HALF).astype(BF16)
        h1, xn, sc = _peer_q(h.reshape(t, d), ypool.reshape(t, D_POOL), yattn.reshape(t, D_ATTN),
                             wo[:D_POOL], wo[D_POOL:], row(ffn_norm_g[i]),
                             peer_wq[i].astype(BF16), sk, tm)
        idx_t, gate_t = _peer_topk(sc, tl)
        idx_flat = idx_t.T.reshape(-1)
        uv = _pack_uv(peer_u[i], peer_v[i])
        out = _peer_ffn(idx_flat, gate_t, xn, h1, uv, p[i].reshape(t, D_PLE), row(ple_norm_g[i]),
                        ple_w_gate[i].astype(BF16), ple_w_proj[i].astype(BF16),
                        row(final_norm_g), i + 1 == depth)
        h = out.reshape(b, s, d)
    return h
```
